```python
import jax
import jax.numpy as jnp
from jax import lax
import numpy as np

D_MODEL = 2048
BATCH = 16
SEQ = 2048
DEPTH = 4
DEC_BATCH = 4
DEC_SEQ = 8192
PAST_LEN = 128

GRID_W = 64
HEAD_DIM = 128
EPS = 1e-6
NEG_INF = -1e30

NA_HEADS = 8
NA_WIN_R = 8
NA_WIN_C = 16
NA_COL_BLOCK = 16
NA_COL_SPAN = 32

MLA_HEADS = 8
MLA_Q_RANK = 512
MLA_KV_RANK = 256
MLA_NOPE = 128
MLA_ROPE = 64
MLA_V = 128
MLA_QK = MLA_NOPE + MLA_ROPE
MLA_Q_BLOCK = 128
ROPE_THETA = 10000.0

DIL_GROUPS = ((128, 1), (512, 4), (2048, 16))
DIL_HEADS_PER_GROUP = 4
DIL_HEADS = 12
DIL_Q_BLOCK = 64

D_FF = 5632

NA_WIDTH = NA_HEADS * HEAD_DIM
DIL_WIDTH = DIL_HEADS * HEAD_DIM
DIL_OUT = DIL_HEADS_PER_GROUP * HEAD_DIM
MLA_OUT = MLA_HEADS * MLA_V
SPLITS = (NA_WIDTH, NA_WIDTH, NA_WIDTH, MLA_Q_RANK, MLA_KV_RANK, MLA_ROPE,
          DIL_WIDTH, DIL_WIDTH, DIL_WIDTH, D_MODEL, D_MODEL, D_MODEL)
P_IN = 3 * NA_WIDTH + MLA_Q_RANK + MLA_KV_RANK + MLA_ROPE + 3 * DIL_WIDTH + 3 * D_MODEL

kernel_name = "hybrid_bidir_encoder_na_mla_dilated"


def rms_norm(x, g):
    xf = x.astype(jnp.float32)
    y = xf * lax.rsqrt(jnp.mean(xf * xf, axis=-1, keepdims=True) + EPS)
    return (y * g.astype(jnp.float32)).astype(x.dtype)


def swiglu(x, w_gate, w_up, w_down):
    return (jax.nn.silu(x @ w_gate) * (x @ w_up)) @ w_down


def to_heads(t, n_heads, dh):
    b, s, _ = t.shape
    return t.reshape(b, s, n_heads, dh).transpose(0, 2, 1, 3)


def from_heads(t):
    b, h, s, d = t.shape
    return t.transpose(0, 2, 1, 3).reshape(b, s, h * d)


def neighbourhood_attention(q, k, v, rpb):
    b, h, s, hd = q.shape
    rows = s // GRID_W
    kr = min(NA_WIN_R, rows)
    n_cb = GRID_W // NA_COL_BLOCK
    qg = q.reshape(b, h, rows, n_cb, NA_COL_BLOCK, hd)
    kg = k.reshape(b, h, rows, GRID_W, hd)
    vg = v.reshape(b, h, rows, GRID_W, hd)
    qcol = np.arange(GRID_W).reshape(n_cb, NA_COL_BLOCK)
    col_start = np.clip(qcol - NA_WIN_C // 2, 0, GRID_W - NA_WIN_C)
    span_start = np.clip(np.arange(n_cb) * NA_COL_BLOCK - NA_WIN_C // 2, 0, GRID_W - NA_COL_SPAN)
    kcol = span_start[:, None] + np.arange(NA_COL_SPAN)[None, :]
    dc = kcol[:, None, :] - qcol[:, :, None]
    col_ok = (kcol[:, None, :] >= col_start[:, :, None]) & (kcol[:, None, :] < col_start[:, :, None] + NA_WIN_C)
    dc_idx = np.clip(dc, -(NA_WIN_C - 1), NA_WIN_C - 1) + NA_WIN_C - 1
    rpb_c = rpb[:, :, dc_idx]
    col_ok_j = jnp.asarray(col_ok)[:, :, None, :]
    kcol_j = jnp.asarray(kcol)
    scale = HEAD_DIM ** -0.5

    def row_block(r):
        r0 = jnp.clip(r - kr // 2, 0, rows - kr)
        k_rows = lax.dynamic_slice_in_dim(kg, r0, kr, axis=2)
        v_rows = lax.dynamic_slice_in_dim(vg, r0, kr, axis=2)
        k_blk = jnp.take(k_rows, kcol_j, axis=3)
        v_blk = jnp.take(v_rows, kcol_j, axis=3)
        q_row = lax.dynamic_index_in_dim(qg, r, axis=2, keepdims=False)
        sc = jnp.einsum('bhcqd,bhicmd->bhcqim', q_row, k_blk).astype(jnp.float32) * scale
        dr_idx = r0 + jnp.arange(kr) - r + NA_WIN_R - 1
        bias = jnp.take(rpb_c, dr_idx, axis=1).transpose(0, 2, 3, 1, 4)
        sc = jnp.where(col_ok_j, sc + bias.astype(jnp.float32), NEG_INF)
        p = jax.nn.softmax(sc.reshape(b, h, n_cb, NA_COL_BLOCK, kr * NA_COL_SPAN), axis=-1)
        p = p.reshape(sc.shape).astype(v.dtype)
        o = jnp.einsum('bhcqim,bhicmd->bhcqd', p, v_blk)
        return o.reshape(b, h, GRID_W, hd)

    out = lax.map(row_block, jnp.arange(rows))
    return out.transpose(1, 2, 0, 3, 4).reshape(b, h, s, hd)


def rope(x, pos):
    half = x.shape[-1] // 2
    inv = ROPE_THETA ** (-jnp.arange(half, dtype=jnp.float32) / half)
    ang = pos.astype(jnp.float32)[:, None] * inv[None, :]
    cos, sin = jnp.cos(ang), jnp.sin(ang)
    xf = x.astype(jnp.float32)
    x1, x2 = xf[..., :half], xf[..., half:]
    return jnp.concatenate([x1 * cos - x2 * sin, x1 * sin + x2 * cos], axis=-1).astype(x.dtype)


def mla_attention(c_q, c_kv, k_rope, cq_norm, ckv_norm, w_uq, w_ukv, q_norm, k_norm):
    b, s, _ = c_q.shape
    pos = jnp.arange(s)
    q = to_heads(rms_norm(c_q, cq_norm) @ w_uq, MLA_HEADS, MLA_QK)
    kv = to_heads(rms_norm(c_kv, ckv_norm) @ w_ukv, MLA_HEADS, MLA_NOPE + MLA_V)
    k_nope, v = kv[..., :MLA_NOPE], kv[..., MLA_NOPE:]
    q = jnp.concatenate([q[..., :MLA_NOPE], rope(q[..., MLA_NOPE:], pos)], axis=-1)
    k_r = jnp.broadcast_to(rope(k_rope, pos)[:, None], (b, MLA_HEADS, s, MLA_ROPE))
    k = jnp.concatenate([k_nope, k_r], axis=-1)
    q = rms_norm(q, q_norm)
    k = rms_norm(k, k_norm)
    n_blk = s // MLA_Q_BLOCK
    qb = q.reshape(b, MLA_HEADS, n_blk, MLA_Q_BLOCK, MLA_QK).transpose(2, 0, 1, 3, 4)
    scale = MLA_QK ** -0.5

    def block(q_blk):
        sc = jnp.einsum('bhqd,bhkd->bhqk', q_blk, k).astype(jnp.float32) * scale
        p = jax.nn.softmax(sc, axis=-1).astype(v.dtype)
        return jnp.einsum('bhqk,bhkd->bhqd', p, v)

    o = lax.map(block, qb)
    return o.transpose(1, 0, 3, 2, 4).reshape(b, s, MLA_OUT)


def alibi_slopes(n):
    return 2.0 ** (-8.0 * jnp.arange(1, n + 1, dtype=jnp.float32) / n)


def banded_attention(q, k, v, slope, radius):
    b, h, r, l, hd = q.shape
    n_blk = -(-l // DIL_Q_BLOCK)
    lp = n_blk * DIL_Q_BLOCK
    kb_len = DIL_Q_BLOCK + 2 * radius
    lead = [(0, 0)] * 3
    qp = jnp.pad(q, lead + [(0, lp - l), (0, 0)])
    kp = jnp.pad(k, lead + [(radius, lp - l + radius), (0, 0)])
    vp = jnp.pad(v, lead + [(radius, lp - l + radius), (0, 0)])
    key_idx = np.arange(n_blk)[:, None] * DIL_Q_BLOCK + np.arange(kb_len)[None, :]
    kb = jnp.take(kp, key_idx, axis=3)
    vb = jnp.take(vp, key_idx, axis=3)
    qb = qp.reshape(b, h, r, n_blk, DIL_Q_BLOCK, hd)
    qpos = np.arange(lp).reshape(n_blk, DIL_Q_BLOCK)
    kpos = key_idx - radius
    dist = np.abs(qpos[:, :, None] - kpos[:, None, :])
    valid = (kpos[:, None, :] >= 0) & (kpos[:, None, :] < l) & (dist <= radius)
    sc = jnp.einsum('bhrnqd,bhrnkd->bhrnqk', qb, kb).astype(jnp.float32) * HEAD_DIM ** -0.5
    sc = sc - slope.astype(jnp.float32)[None, :, None, None, None, None] * jnp.asarray(dist, jnp.float32)
    sc = jnp.where(jnp.asarray(valid), sc, NEG_INF)
    m = jnp.max(sc, axis=-1, keepdims=True)
    p = jnp.exp(sc - m)
    den = jnp.sum(p, axis=-1, keepdims=True)
    o = jnp.einsum('bhrnqk,bhrnkd->bhrnqd', p.astype(v.dtype), vb).astype(jnp.float32) / den
    lse = (m + jnp.log(den))[..., 0]
    o = o.reshape(b, h, r, lp, hd)[:, :, :, :l]
    lse = lse.reshape(b, h, r, lp)[:, :, :, :l]
    return o, lse


def dilated_attention(q, k, v):
    b, h, s, hd = q.shape
    slopes = alibi_slopes(DIL_HEADS)
    outs, lses = [], []
    for g, (window, dil) in enumerate(DIL_GROUPS):
        lo, hi = g * DIL_HEADS_PER_GROUP, (g + 1) * DIL_HEADS_PER_GROUP
        n = s // dil
        qr = q[:, lo:hi].reshape(b, DIL_HEADS_PER_GROUP, n, dil, hd).transpose(0, 1, 3, 2, 4)
        kr = k[:, lo:hi].reshape(b, DIL_HEADS_PER_GROUP, n, dil, hd).transpose(0, 1, 3, 2, 4)
        vr = v[:, lo:hi].reshape(b, DIL_HEADS_PER_GROUP, n, dil, hd).transpose(0, 1, 3, 2, 4)
        o, lse = banded_attention(qr, kr, vr, slopes[lo:hi] * dil, (window // 2) // dil)
        outs.append(o.transpose(0, 1, 3, 2, 4).reshape(b, DIL_HEADS_PER_GROUP, s, hd))
        lses.append(lse.transpose(0, 1, 3, 2).reshape(b, DIL_HEADS_PER_GROUP, s))
    wts = jax.nn.softmax(jnp.stack(lses), axis=0)
    o = jnp.sum(wts[..., None] * jnp.stack(outs), axis=0)
    return from_heads(o).astype(q.dtype)


def token_mixing(h, w_in, na_q_norm, na_k_norm, na_rpb, mla_cq_norm, mla_ckv_norm, mla_w_uq,
                 mla_w_ukv, mla_q_norm, mla_k_norm, dil_q_norm, dil_k_norm,
                 w_na_out, w_mla_out, w_dil_out, w_o):
    z = h @ w_in
    (na_q, na_k, na_v, c_q, c_kv, k_rope, dq, dk, dv, g_na, g_mla, g_dil) = jnp.split(
        z, np.cumsum(SPLITS)[:-1].tolist(), axis=-1)
    qa = rms_norm(to_heads(na_q, NA_HEADS, HEAD_DIM), na_q_norm)
    ka = rms_norm(to_heads(na_k, NA_HEADS, HEAD_DIM), na_k_norm)
    va = to_heads(na_v, NA_HEADS, HEAD_DIM)
    y_na = from_heads(neighbourhood_attention(qa, ka, va, na_rpb)) @ w_na_out
    y_mla = mla_attention(c_q, c_kv, k_rope, mla_cq_norm, mla_ckv_norm, mla_w_uq, mla_w_ukv,
                          mla_q_norm, mla_k_norm) @ w_mla_out
    qd = rms_norm(to_heads(dq, DIL_HEADS, HEAD_DIM), dil_q_norm)
    kd = rms_norm(to_heads(dk, DIL_HEADS, HEAD_DIM), dil_k_norm)
    vd = to_heads(dv, DIL_HEADS, HEAD_DIM)
    y_dil = dilated_attention(qd, kd, vd) @ w_dil_out
    merged = jax.nn.sigmoid(g_na) * y_na + jax.nn.sigmoid(g_mla) * y_mla + jax.nn.sigmoid(g_dil) * y_dil
    return merged @ w_o


def encoder_layer(x, ffn1_norm, ffn1_w_gate, ffn1_w_up, ffn1_w_down, mix_norm, w_in,
                  na_q_norm, na_k_norm, na_rpb, mla_cq_norm, mla_ckv_norm, mla_w_uq, mla_w_ukv,
                  mla_q_norm, mla_k_norm, dil_q_norm, dil_k_norm, w_na_out, w_mla_out, w_dil_out,
                  w_o, ffn2_norm, ffn2_w_gate, ffn2_w_up, ffn2_w_down):
    x = x + 0.5 * swiglu(rms_norm(x, ffn1_norm), ffn1_w_gate, ffn1_w_up, ffn1_w_down)
    x = x + token_mixing(rms_norm(x, mix_norm), w_in, na_q_norm, na_k_norm, na_rpb,
                         mla_cq_norm, mla_ckv_norm, mla_w_uq, mla_w_ukv, mla_q_norm, mla_k_norm,
                         dil_q_norm, dil_k_norm, w_na_out, w_mla_out, w_dil_out, w_o)
    x = x + 0.5 * swiglu(rms_norm(x, ffn2_norm), ffn2_w_gate, ffn2_w_up, ffn2_w_down)
    return x


def trunk(x, params):
    for layer in range(DEPTH):
        x = encoder_layer(x, *[p[layer] for p in params])
    return x


def setup_inputs(seed: int = 0) -> dict:
    key = jax.random.key(seed)
    ks = jax.random.split(key, 32)
    f32 = jnp.float32
    L = DEPTH

    def w(k, shape, fan_in):
        return jax.random.normal(k, shape, f32) * fan_in ** -0.5

    def gain(k, shape):
        return 1.0 + 0.02 * jax.random.normal(k, shape, f32)

    return {
        "x_prompt": jax.random.normal(ks[0], (BATCH, SEQ, D_MODEL), f32),
        "x_sample": jax.random.normal(ks[1], (DEC_BATCH, DEC_SEQ, D_MODEL), f32),
        "ffn1_norm": gain(ks[2], (L, D_MODEL)),
        "ffn1_w_gate": w(ks[3], (L, D_MODEL, D_FF), D_MODEL),
        "ffn1_w_up": w(ks[4], (L, D_MODEL, D_FF), D_MODEL),
        "ffn1_w_down": w(ks[5], (L, D_FF, D_MODEL), D_FF),
        "mix_norm": gain(ks[6], (L, D_MODEL)),
        "w_in": w(ks[7], (L, D_MODEL, P_IN), D_MODEL),
        "na_q_norm": gain(ks[8], (L, HEAD_DIM)),
        "na_k_norm": gain(ks[9], (L, HEAD_DIM)),
        "na_rpb": 0.1 * jax.random.normal(ks[10], (L, NA_HEADS, 2 * NA_WIN_R - 1, 2 * NA_WIN_C - 1), f32),
        "mla_cq_norm": gain(ks[11], (L, MLA_Q_RANK)),
        "mla_ckv_norm": gain(ks[12], (L, MLA_KV_RANK)),
        "mla_w_uq": w(ks[13], (L, MLA_Q_RANK, MLA_HEADS * MLA_QK), MLA_Q_RANK),
        "mla_w_ukv": w(ks[14], (L, MLA_KV_RANK, MLA_HEADS * (MLA_NOPE + MLA_V)), MLA_KV_RANK),
        "mla_q_norm": gain(ks[15], (L, MLA_QK)),
        "mla_k_norm": gain(ks[16], (L, MLA_QK)),
        "dil_q_norm": gain(ks[17], (L, HEAD_DIM)),
        "dil_k_norm": gain(ks[18], (L, HEAD_DIM)),
        "w_na_out": w(ks[19], (L, NA_WIDTH, D_MODEL), NA_WIDTH),
        "w_mla_out": w(ks[20], (L, MLA_OUT, D_MODEL), MLA_OUT),
        "w_dil_out": w(ks[21], (L, DIL_OUT, D_MODEL), DIL_OUT),
        "w_o": w(ks[22], (L, D_MODEL, D_MODEL), D_MODEL),
        "ffn2_norm": gain(ks[23], (L, D_MODEL)),
        "ffn2_w_gate": w(ks[24], (L, D_MODEL, D_FF), D_MODEL),
        "ffn2_w_up": w(ks[25], (L, D_MODEL, D_FF), D_MODEL),
        "ffn2_w_down": w(ks[26], (L, D_FF, D_MODEL), D_FF),
    }


def reference(x_prompt, x_sample, ffn1_norm, ffn1_w_gate, ffn1_w_up, ffn1_w_down, mix_norm, w_in,
              na_q_norm, na_k_norm, na_rpb, mla_cq_norm, mla_ckv_norm, mla_w_uq, mla_w_ukv,
              mla_q_norm, mla_k_norm, dil_q_norm, dil_k_norm, w_na_out, w_mla_out, w_dil_out,
              w_o, ffn2_norm, ffn2_w_gate, ffn2_w_up, ffn2_w_down):
    params = (ffn1_norm, ffn1_w_gate, ffn1_w_up, ffn1_w_down, mix_norm, w_in,
              na_q_norm, na_k_norm, na_rpb, mla_cq_norm, mla_ckv_norm, mla_w_uq, mla_w_ukv,
              mla_q_norm, mla_k_norm, dil_q_norm, dil_k_norm, w_na_out, w_mla_out, w_dil_out,
              w_o, ffn2_norm, ffn2_w_gate, ffn2_w_up, ffn2_w_down)
    y_prompt = trunk(x_prompt, params)
    y_sample = trunk(x_sample, params)
    return (y_prompt, y_sample)
```

```python
import functools

import numpy as np
import jax
import jax.numpy as jnp
from jax import lax
from jax.experimental import pallas as pl
from jax.experimental.pallas import tpu as pltpu

F32 = jnp.float32
BF16 = jnp.bfloat16

D_MODEL = 2048
GRID_W = 64
HEAD_DIM = 128
EPS = 1e-6
NEG_INF = -1e30

NA_HEADS = 8
NA_WIN_R = 8
NA_WIN_C = 16
NA_WIDTH = NA_HEADS * HEAD_DIM

MLA_HEADS = 8
MLA_Q_RANK = 512
MLA_KV_RANK = 256
MLA_NOPE = 128
MLA_ROPE = 64
MLA_V = 128
MLA_QK = MLA_NOPE + MLA_ROPE
MLA_OUT = MLA_HEADS * MLA_V
ROPE_THETA = 10000.0

DIL_GROUPS = ((128, 1), (512, 4), (2048, 16))
DIL_HEADS_PER_GROUP = 4
DIL_HEADS = 12
DIL_WIDTH = DIL_HEADS * HEAD_DIM
DIL_OUT = DIL_HEADS_PER_GROUP * HEAD_DIM

D_FF = 5632

LANES = 128
VMEM_LIMIT = 56 * 1024 * 1024

QKV_COLS = 2 * NA_WIDTH + 2 * DIL_WIDTH + NA_WIDTH + DIL_WIDTH
QKV_NORMED_COLS = 2 * NA_WIDTH + 2 * DIL_WIDTH
LAT_COLS = MLA_Q_RANK + MLA_KV_RANK + 2 * LANES
AUX_COLS = LAT_COLS + 3 * D_MODEL
MLA_QPAD = 2 * LANES
MLA_QEXT = 3 * LANES


def _params(*sem):
    return pltpu.CompilerParams(dimension_semantics=sem, vmem_limit_bytes=VMEM_LIMIT)


def _rms(x, gain):
    return x * lax.rsqrt(jnp.mean(x * x, axis=-1, keepdims=True) + EPS) * gain


def _dot(a, b):
    return jnp.dot(a, b, preferred_element_type=F32)


def _dot_t(a, b):
    return lax.dot_general(a, b, (((1,), (1,)), ((), ())), preferred_element_type=F32)


def _ffn_kernel(x_ref, g_ref, wg_ref, wu_ref, wd_ref, o_ref, xn_ref):
    f = pl.program_id(1)

    @pl.when(f == 0)
    def _():
        x = x_ref[...]
        xn_ref[...] = _rms(x, g_ref[...]).astype(BF16)
        o_ref[...] = x

    xn = xn_ref[...]
    g = _dot(xn, wg_ref[...])
    u = _dot(xn, wu_ref[...])
    h = (g * jax.nn.sigmoid(g) * u * 0.5).astype(BF16)
    o_ref[...] += _dot(h, wd_ref[...])


def _ffn(x, gain, wg, wu, wd, layer, *, tm=512, tf=512):
    t, d = x.shape
    ff = wg.shape[-1]
    return pl.pallas_call(
        _ffn_kernel,
        grid=(t // tm, ff // tf),
        in_specs=[
            pl.BlockSpec((tm, d), lambda i, f: (i, 0)),
            pl.BlockSpec((None, 1, d), lambda i, f: (layer, 0, 0)),
            pl.BlockSpec((None, d, tf), lambda i, f: (layer, 0, f)),
            pl.BlockSpec((None, d, tf), lambda i, f: (layer, 0, f)),
            pl.BlockSpec((None, tf, d), lambda i, f: (layer, f, 0)),
        ],
        out_specs=pl.BlockSpec((tm, d), lambda i, f: (i, 0)),
        out_shape=jax.ShapeDtypeStruct((t, d), F32),
        scratch_shapes=[pltpu.VMEM((tm, d), BF16)],
        compiler_params=_params("parallel", "arbitrary"),
    )(x, gain, wg, wu, wd)


def _inproj_kernel(x_ref, g_ref, w_ref, hg_ref, o_ref, xn_ref, *, normed_tiles):
    j = pl.program_id(1)

    @pl.when(j == 0)
    def _():
        xn_ref[...] = _rms(x_ref[...], g_ref[...]).astype(BF16)

    acc = _dot(xn_ref[...], w_ref[...])
    tn = acc.shape[1]

    if normed_tiles == 0:
        o_ref[...] = acc.astype(o_ref.dtype)
        return

    @pl.when(j < normed_tiles)
    def _():
        hg = hg_ref[...]
        for c in range(tn // HEAD_DIM):
            sl = slice(c * HEAD_DIM, (c + 1) * HEAD_DIM)
            o_ref[:, sl] = _rms(acc[:, sl], hg[:, sl]).astype(o_ref.dtype)

    @pl.when(j >= normed_tiles)
    def _():
        o_ref[...] = acc.astype(o_ref.dtype)


def _inproj(x, gain, w, head_gain, layer, out_dtype, *, normed_cols, tm=512, tn=512):
    t, d = x.shape
    n = w.shape[-1]
    kern = functools.partial(_inproj_kernel, normed_tiles=normed_cols // tn)
    return pl.pallas_call(
        kern,
        grid=(t // tm, n // tn),
        in_specs=[
            pl.BlockSpec((tm, d), lambda i, j: (i, 0)),
            pl.BlockSpec((None, 1, d), lambda i, j: (layer, 0, 0)),
            pl.BlockSpec((None, d, tn), lambda i, j: (layer, 0, j)),
            pl.BlockSpec((None, 1, tn), lambda i, j: (layer, 0, j)),
        ],
        out_specs=pl.BlockSpec((tm, tn), lambda i, j: (i, j)),
        out_shape=jax.ShapeDtypeStruct((t, n), out_dtype),
        scratch_shapes=[pltpu.VMEM((tm, d), BF16)],
        compiler_params=_params("parallel", "arbitrary"),
    )(x, gain, w, head_gain)


NA_QROWS = 4
NA_TQ = NA_QROWS * GRID_W
NA_KROWS = 3 * NA_QROWS
NA_TK = NA_KROWS * GRID_W


def _na_bias_tables(rpb):
    j = np.arange(NA_QROWS)[:, None, None, None]
    qc = np.arange(GRID_W)[None, :, None, None]
    w = np.arange(NA_KROWS)[None, None, :, None]
    kc = np.arange(GRID_W)[None, None, None, :]
    col_start = np.clip(qc - NA_WIN_C // 2, 0, GRID_W - NA_WIN_C)
    col_ok = (kc >= col_start) & (kc < col_start + NA_WIN_C)
    dc_idx = np.clip(kc - qc, -(NA_WIN_C - 1), NA_WIN_C - 1) + NA_WIN_C - 1
    tables = []
    for q_rel, k0_rel in ((j, 0 * j), (NA_QROWS + j, j), (2 * NA_QROWS + j, NA_QROWS + 0 * j)):
        row_ok = (w >= k0_rel) & (w < k0_rel + NA_WIN_R)
        dr_idx = np.clip(w - q_rel + NA_WIN_R - 1, 0, 2 * NA_WIN_R - 2)
        ok = np.broadcast_to(row_ok & col_ok, (NA_QROWS, GRID_W, NA_KROWS, GRID_W))
        dr_b = np.broadcast_to(dr_idx, ok.shape)
        dc_b = np.broadcast_to(dc_idx, ok.shape)
        bias = rpb[:, dr_b, dc_b]
        bias = jnp.where(jnp.asarray(ok)[None], bias, NEG_INF)
        tables.append(bias.reshape(rpb.shape[0], NA_TQ, NA_TK))
    return jnp.stack(tables)


def _na_kernel(q_ref, k0_ref, k1_ref, k2_ref, v0_ref, v1_ref, v2_ref, bias_ref, o_ref):
    scale = HEAD_DIM ** -0.5
    for h in range(NA_HEADS):
        sl = slice(h * HEAD_DIM, (h + 1) * HEAD_DIM)
        q = q_ref[:, sl]
        s = jnp.concatenate([_dot_t(q, k_ref[:, sl]) for k_ref in (k0_ref, k1_ref, k2_ref)], axis=1)
        s = s * scale + bias_ref[h]
        m = jnp.max(s, axis=-1, keepdims=True)
        e = jnp.exp(s - m)
        p = (e * (1.0 / jnp.sum(e, axis=-1, keepdims=True))).astype(BF16)
        o = _dot(p[:, 0:NA_TQ], v0_ref[:, sl])
        o += _dot(p[:, NA_TQ:2 * NA_TQ], v1_ref[:, sl])
        o += _dot(p[:, 2 * NA_TQ:3 * NA_TQ], v2_ref[:, sl])
        o_ref[:, sl] = o.astype(o_ref.dtype)


def _na_attention(qkv, bias, b, s):
    nb = s // NA_TQ
    assert nb >= 3
    q_col, k_col, v_col = 0, 1, (2 * NA_WIDTH + 2 * DIL_WIDTH) // NA_WIDTH

    def win(i):
        return jnp.clip(i - 1, 0, nb - 3)

    def kv_spec(col, t):
        return pl.BlockSpec((NA_TQ, NA_WIDTH), lambda bi, i: (bi * nb + win(i) + t, col))

    def bias_map(bi, i):
        kind = jnp.where(i == 0, 0, jnp.where(i == nb - 1, 2, 1))
        return (kind, 0, 0, 0)

    return pl.pallas_call(
        _na_kernel,
        grid=(b, nb),
        in_specs=[pl.BlockSpec((NA_TQ, NA_WIDTH), lambda bi, i: (bi * nb + i, q_col))]
        + [kv_spec(k_col, t) for t in range(3)]
        + [kv_spec(v_col, t) for t in range(3)]
        + [pl.BlockSpec((None, NA_HEADS, NA_TQ, NA_TK), bias_map)],
        out_specs=pl.BlockSpec((NA_TQ, NA_WIDTH), lambda bi, i: (bi * nb + i, 0)),
        out_shape=jax.ShapeDtypeStruct((b * s, NA_WIDTH), BF16),
        compiler_params=_params("parallel", "arbitrary"),
    )(qkv, qkv, qkv, qkv, qkv, qkv, qkv, bias)


def _rope_tables(s):
    half = MLA_ROPE // 2
    inv = ROPE_THETA ** (-jnp.arange(half, dtype=F32) / half)
    ang = jnp.arange(s).astype(F32)[:, None] * inv[None, :]
    cos, sin = jnp.cos(ang), jnp.sin(ang)
    zero = jnp.zeros((s, LANES - MLA_ROPE), F32)
    return (jnp.concatenate([cos, cos, zero], axis=-1), jnp.concatenate([-sin, sin, zero], axis=-1))


def _mla_prep_kernel(cq_ref, ckv_ref, kr_ref, krr_ref, cos_ref, sin_ref, cqg_ref, ckvg_ref, wuq_ref, wukv_ref,
                     qg_ref, kg_ref, q_out, k_out, v_out):
    cos = cos_ref[...]
    sin = sin_ref[...]
    qf = _dot(_rms(cq_ref[...], cqg_ref[...]).astype(BF16), wuq_ref[...])
    kvf = _dot(_rms(ckv_ref[...], ckvg_ref[...]).astype(BF16), wukv_ref[...])
    kr = kr_ref[...] * cos + krr_ref[...] * sin
    kr_ss = jnp.sum(kr * kr, axis=-1, keepdims=True)
    qg = qg_ref[...]
    kg = kg_ref[...]
    for h in range(MLA_HEADS):
        qb = h * MLA_QEXT
        qn = qf[:, qb:qb + LANES]
        qr = qf[:, qb + LANES:qb + 2 * LANES] * cos + qf[:, qb + 2 * LANES:qb + 3 * LANES] * sin
        ms = (jnp.sum(qn * qn, axis=-1, keepdims=True) + jnp.sum(qr * qr, axis=-1, keepdims=True)) * (1.0 / MLA_QK)
        r = lax.rsqrt(ms + EPS)
        ob = h * MLA_QPAD
        q_out[:, ob:ob + LANES] = (qn * r * qg[:, :LANES]).astype(BF16)
        q_out[:, ob + LANES:ob + 2 * LANES] = (qr * r * qg[:, LANES:]).astype(BF16)
        kb = h * (MLA_NOPE + MLA_V)
        kn = kvf[:, kb:kb + MLA_NOPE]
        msk = (jnp.sum(kn * kn, axis=-1, keepdims=True) + kr_ss) * (1.0 / MLA_QK)
        rk = lax.rsqrt(msk + EPS)
        k_out[:, ob:ob + LANES] = (kn * rk * kg[:, :LANES]).astype(BF16)
        k_out[:, ob + LANES:ob + 2 * LANES] = (kr * rk * kg[:, LANES:]).astype(BF16)
        v_out[:, h * MLA_V:(h + 1) * MLA_V] = kvf[:, kb + MLA_NOPE:kb + MLA_NOPE + MLA_V].astype(BF16)


def _mla_prep(aux, cos, sin, cq_gain, ckv_gain, wuq, wukv, q_gain, k_gain, layer, s, *, tm=256):
    t = aux.shape[0]
    n_pos = s // tm
    full = lambda shape: pl.BlockSpec((None,) + shape, lambda i: (layer,) + (0,) * len(shape))
    return pl.pallas_call(
        _mla_prep_kernel,
        grid=(t // tm,),
        in_specs=[
            pl.BlockSpec((tm, MLA_Q_RANK), lambda i: (i, 0)),
            pl.BlockSpec((tm, MLA_KV_RANK), lambda i: (i, MLA_Q_RANK // MLA_KV_RANK)),
            pl.BlockSpec((tm, LANES), lambda i: (i, (MLA_Q_RANK + MLA_KV_RANK) // LANES)),
            pl.BlockSpec((tm, LANES), lambda i: (i, (MLA_Q_RANK + MLA_KV_RANK) // LANES + 1)),
            pl.BlockSpec((tm, LANES), lambda i: (i % n_pos, 0)),
            pl.BlockSpec((tm, LANES), lambda i: (i % n_pos, 0)),
            full((1, MLA_Q_RANK)),
            full((1, MLA_KV_RANK)),
            full((MLA_Q_RANK, MLA_HEADS * MLA_QEXT)),
            full((MLA_KV_RANK, MLA_HEADS * (MLA_NOPE + MLA_V))),
            full((1, MLA_QPAD)),
            full((1, MLA_QPAD)),
        ],
        out_specs=[
            pl.BlockSpec((tm, MLA_HEADS * MLA_QPAD), lambda i: (i, 0)),
            pl.BlockSpec((tm, MLA_HEADS * MLA_QPAD), lambda i: (i, 0)),
            pl.BlockSpec((tm, MLA_OUT), lambda i: (i, 0)),
        ],
        out_shape=[
            jax.ShapeDtypeStruct((t, MLA_HEADS * MLA_QPAD), BF16),
            jax.ShapeDtypeStruct((t, MLA_HEADS * MLA_QPAD), BF16),
            jax.ShapeDtypeStruct((t, MLA_OUT), BF16),
        ],
        compiler_params=_params("parallel"),
    )(aux, aux, aux, aux, cos, sin, cq_gain, ckv_gain, wuq, wukv, q_gain, k_gain)


def _mla_flash_kernel(q_ref, k_ref, v_ref, o_ref, m_ref, l_ref, acc_ref):
    ki = pl.program_id(3)

    @pl.when(ki == 0)
    def _():
        m_ref[...] = jnp.full_like(m_ref, -jnp.inf)
        l_ref[...] = jnp.zeros_like(l_ref)
        acc_ref[...] = jnp.zeros_like(acc_ref)

    s = _dot_t(q_ref[...], k_ref[...]) * (MLA_QK ** -0.5)
    m_prev = m_ref[...]
    m_new = jnp.maximum(m_prev, jnp.max(s, axis=-1, keepdims=True))
    alpha = jnp.exp(m_prev - m_new)
    p = jnp.exp(s - m_new)
    l_ref[...] = alpha * l_ref[...] + jnp.sum(p, axis=-1, keepdims=True)
    acc_ref[...] = alpha * acc_ref[...] + _dot(p.astype(BF16), v_ref[...])
    m_ref[...] = m_new

    @pl.when(ki == pl.num_programs(3) - 1)
    def _():
        o_ref[...] = (acc_ref[...] * (1.0 / l_ref[...])).astype(o_ref.dtype)


def _mla_attention(q, k, v, b, s, *, tq=512, tk=1024):
    nq, nk = s // tq, s // tk
    return pl.pallas_call(
        _mla_flash_kernel,
        grid=(b, MLA_HEADS, nq, nk),
        in_specs=[
            pl.BlockSpec((tq, MLA_QPAD), lambda bi, h, qi, ki: (bi * nq + qi, h)),
            pl.BlockSpec((tk, MLA_QPAD), lambda bi, h, qi, ki: (bi * nk + ki, h)),
            pl.BlockSpec((tk, MLA_V), lambda bi, h, qi, ki: (bi * nk + ki, h)),
        ],
        out_specs=pl.BlockSpec((tq, MLA_V), lambda bi, h, qi, ki: (bi * nq + qi, h)),
        out_shape=jax.ShapeDtypeStruct((b * s, MLA_OUT), BF16),
        scratch_shapes=[pltpu.VMEM((tq, 1), F32), pltpu.VMEM((tq, 1), F32), pltpu.VMEM((tq, MLA_V), F32)],
        compiler_params=_params("parallel", "parallel", "parallel", "arbitrary"),
    )(q, k, v)


def _alibi_slopes(n):
    return 2.0 ** (-8.0 * jnp.arange(1, n + 1, dtype=F32) / n)


def _dil_kernel(slope_ref, q_ref, k0_ref, k1_ref, k2_ref, v0_ref, v1_ref, v2_ref, o_ref, lse_ref, *, tu, n, radius):
    i = pl.program_id(2)
    row = lax.broadcasted_iota(jnp.int32, (tu, 3 * tu), 0)
    col = lax.broadcasted_iota(jnp.int32, (tu, 3 * tu), 1)
    kpos = (i - 1) * tu + col
    dist_i = jnp.abs(row + tu - col)
    valid = (kpos >= 0) & (kpos < n) & (dist_i <= radius)
    dist = dist_i.astype(F32)
    scale = HEAD_DIM ** -0.5
    for h in range(DIL_HEADS_PER_GROUP):
        sl = slice(h * HEAD_DIM, (h + 1) * HEAD_DIM)
        q = q_ref[:, sl]
        s = jnp.concatenate([_dot_t(q, k_ref[:, sl]) for k_ref in (k0_ref, k1_ref, k2_ref)], axis=1)
        s = s * scale - slope_ref[h] * dist
        s = jnp.where(valid, s, NEG_INF)
        m = jnp.max(s, axis=-1, keepdims=True)
        p = jnp.exp(s - m)
        den = jnp.sum(p, axis=-1, keepdims=True)
        pb = p.astype(BF16)
        o = _dot(pb[:, 0:tu], v0_ref[:, sl])
        o += _dot(pb[:, tu:2 * tu], v1_ref[:, sl])
        o += _dot(pb[:, 2 * tu:3 * tu], v2_ref[:, sl])
        o_ref[:, sl] = o * (1.0 / den)
        lse_ref[:, sl] = jnp.broadcast_to(m + jnp.log(den), (tu, HEAD_DIM))


def _dil_group_attention(qkv, slopes, group, b, s):
    window, dil = DIL_GROUPS[group]
    radius = (window // 2) // dil
    n = s // dil
    tu = min(128, n)
    nblk = n // tu
    assert radius <= tu
    t = b * s
    units = QKV_COLS // DIL_OUT
    q_unit = (2 * NA_WIDTH) // DIL_OUT + group
    k_unit = q_unit + DIL_WIDTH // DIL_OUT
    v_unit = (QKV_NORMED_COLS + NA_WIDTH) // DIL_OUT + group
    qkv_v = qkv.reshape(t // dil, dil * QKV_COLS)

    def kv_spec(unit, off):
        return pl.BlockSpec(
            (tu, DIL_OUT),
            lambda bi, r, i: (bi * nblk + jnp.clip(i + off, 0, nblk - 1), r * units + unit))

    out_spec = pl.BlockSpec((tu, DIL_OUT), lambda bi, r, i: (bi * nblk + i, r))
    kern = functools.partial(_dil_kernel, tu=tu, n=n, radius=radius)
    o, lse = pl.pallas_call(
        kern,
        grid=(b, dil, nblk),
        in_specs=[pl.BlockSpec(memory_space=pltpu.SMEM),
                  pl.BlockSpec((tu, DIL_OUT), lambda bi, r, i: (bi * nblk + i, r * units + q_unit))]
        + [kv_spec(k_unit, off) for off in (-1, 0, 1)]
        + [kv_spec(v_unit, off) for off in (-1, 0, 1)],
        out_specs=[out_spec, out_spec],
        out_shape=[jax.ShapeDtypeStruct((t // dil, dil * DIL_OUT), F32)] * 2,
        compiler_params=_params("parallel", "parallel", "arbitrary"),
    )(slopes, qkv_v, qkv_v, qkv_v, qkv_v, qkv_v, qkv_v, qkv_v)
    return o.reshape(t, DIL_OUT), lse.reshape(t, DIL_OUT)


def _merge_kernel(ona_ref, omla_ref, od0_ref, od1_ref, od2_ref, ls0_ref, ls1_ref, ls2_ref,
                  gna_ref, gmla_ref, gdil_ref, wna_ref, wmla_ref, wdil_ref, o_ref, od_ref):
    j = pl.program_id(1)

    @pl.when(j == 0)
    def _():
        ls0, ls1, ls2 = ls0_ref[...], ls1_ref[...], ls2_ref[...]
        top = jnp.maximum(jnp.maximum(ls0, ls1), ls2)
        w0, w1, w2 = jnp.exp(ls0 - top), jnp.exp(ls1 - top), jnp.exp(ls2 - top)
        inv = 1.0 / (w0 + w1 + w2)
        od = (w0 * inv) * od0_ref[...] + (w1 * inv) * od1_ref[...] + (w2 * inv) * od2_ref[...]
        od_ref[...] = od.astype(BF16)

    merged = jax.nn.sigmoid(gna_ref[...]) * _dot(ona_ref[...], wna_ref[...])
    merged += jax.nn.sigmoid(gmla_ref[...]) * _dot(omla_ref[...], wmla_ref[...])
    merged += jax.nn.sigmoid(gdil_ref[...]) * _dot(od_ref[...], wdil_ref[...])
    o_ref[...] = merged.astype(o_ref.dtype)


def _merge(o_na, o_mla, o_dil, lse_dil, aux, w_na, w_mla, w_dil, layer, *, tm=512, tn=512):
    t = o_na.shape[0]
    gate0 = LAT_COLS // tn
    gstep = D_MODEL // tn
    row = lambda width: pl.BlockSpec((tm, width), lambda i, j: (i, 0))
    gate = lambda g: pl.BlockSpec((tm, tn), lambda i, j: (i, gate0 + g * gstep + j))
    wcol = lambda k: pl.BlockSpec((None, k, tn), lambda i, j: (layer, 0, j))
    return pl.pallas_call(
        _merge_kernel,
        grid=(t // tm, D_MODEL // tn),
        in_specs=[row(NA_WIDTH), row(MLA_OUT)] + [row(DIL_OUT)] * 6 + [gate(0), gate(1), gate(2)]
        + [wcol(NA_WIDTH), wcol(MLA_OUT), wcol(DIL_OUT)],
        out_specs=pl.BlockSpec((tm, tn), lambda i, j: (i, j)),
        out_shape=jax.ShapeDtypeStruct((t, D_MODEL), BF16),
        scratch_shapes=[pltpu.VMEM((tm, DIL_OUT), BF16)],
        compiler_params=_params("parallel", "arbitrary"),
    )(o_na, o_mla, *o_dil, *lse_dil, aux, aux, aux, w_na, w_mla, w_dil)


def _resid_matmul_kernel(a_ref, w_ref, x_ref, o_ref):
    o_ref[...] = x_ref[...] + _dot(a_ref[...], w_ref[...])


def _resid_matmul(a, w, x, layer, *, tm=512, tn=512):
    t, k = a.shape
    n = w.shape[-1]
    return pl.pallas_call(
        _resid_matmul_kernel,
        grid=(t // tm, n // tn),
        in_specs=[
            pl.BlockSpec((tm, k), lambda i, j: (i, 0)),
            pl.BlockSpec((None, k, tn), lambda i, j: (layer, 0, j)),
            pl.BlockSpec((tm, tn), lambda i, j: (i, j)),
        ],
        out_specs=pl.BlockSpec((tm, tn), lambda i, j: (i, j)),
        out_shape=jax.ShapeDtypeStruct((t, n), F32),
        compiler_params=_params("parallel", "arbitrary"),
    )(a, w, x)


def _prepare(p):
    depth = p["w_in"].shape[0]
    row = lambda g: g[:, None, :].astype(F32)
    w_in = p["w_in"]
    o = np.cumsum((0, NA_WIDTH, NA_WIDTH, NA_WIDTH, MLA_Q_RANK, MLA_KV_RANK, MLA_ROPE,
                   DIL_WIDTH, DIL_WIDTH, DIL_WIDTH, D_MODEL, D_MODEL, D_MODEL))
    seg = lambda a: w_in[:, :, o[a]:o[a + 1]]
    rot = np.concatenate([np.arange(MLA_ROPE // 2, MLA_ROPE), np.arange(MLA_ROPE // 2)])
    zpad = jnp.zeros((depth, D_MODEL, LANES - MLA_ROPE), w_in.dtype)
    w_qkv = jnp.concatenate([seg(0), seg(1), seg(6), seg(7), seg(2), seg(8)], axis=-1).astype(BF16)
    w_aux = jnp.concatenate([seg(3), seg(4), seg(5), zpad, seg(5)[:, :, rot], zpad, seg(9), seg(10), seg(11)],
                            axis=-1).astype(BF16)
    tile = lambda g, reps: jnp.tile(g, (1, reps))
    head_gain = jnp.concatenate(
        [tile(p["na_q_norm"], NA_HEADS), tile(p["na_k_norm"], NA_HEADS),
         tile(p["dil_q_norm"], DIL_HEADS), tile(p["dil_k_norm"], DIL_HEADS),
         jnp.ones((depth, NA_WIDTH + DIL_WIDTH), F32)], axis=-1)

    wuq = p["mla_w_uq"].reshape(depth, MLA_Q_RANK, MLA_HEADS, MLA_QK)
    zq = jnp.zeros((depth, MLA_Q_RANK, MLA_HEADS, LANES - MLA_ROPE), wuq.dtype)
    wuq_rope = wuq[..., MLA_NOPE:]
    wuq = jnp.concatenate([wuq[..., :MLA_NOPE], wuq_rope, zq, wuq_rope[..., rot], zq], axis=-1)
    pad_gain = lambda g: jnp.concatenate([g, jnp.zeros((depth, MLA_QPAD - MLA_QK), F32)], axis=-1)

    return dict(
        ffn1=(row(p["ffn1_norm"]), p["ffn1_w_gate"].astype(BF16), p["ffn1_w_up"].astype(BF16),
              p["ffn1_w_down"].astype(BF16)),
        ffn2=(row(p["ffn2_norm"]), p["ffn2_w_gate"].astype(BF16), p["ffn2_w_up"].astype(BF16),
              p["ffn2_w_down"].astype(BF16)),
        mix_norm=row(p["mix_norm"]),
        w_qkv=w_qkv, w_aux=w_aux, head_gain=row(head_gain),
        aux_gain=jnp.ones((depth, 1, AUX_COLS), F32),
        na_bias=[_na_bias_tables(p["na_rpb"][l]) for l in range(depth)],
        cq_gain=row(p["mla_cq_norm"]), ckv_gain=row(p["mla_ckv_norm"]),
        wuq=wuq.reshape(depth, MLA_Q_RANK, MLA_HEADS * MLA_QEXT).astype(BF16),
        wukv=p["mla_w_ukv"].astype(BF16),
        q_gain=row(pad_gain(p["mla_q_norm"])), k_gain=row(pad_gain(p["mla_k_norm"])),
        w_na=p["w_na_out"].astype(BF16), w_mla=p["w_mla_out"].astype(BF16), w_dil=p["w_dil_out"].astype(BF16),
        w_o=p["w_o"].astype(BF16),
    )


def _dil_slopes():
    slopes = _alibi_slopes(DIL_HEADS)
    out = []
    for g, (_, dil) in enumerate(DIL_GROUPS):
        lo, hi = g * DIL_HEADS_PER_GROUP, (g + 1) * DIL_HEADS_PER_GROUP
        out.append(slopes[lo:hi] * dil)
    return out


def _layer(x, w, layer, b, s, rope, slopes):
    x = _ffn(x, *w["ffn1"], layer)
    qkv = _inproj(x, w["mix_norm"], w["w_qkv"], w["head_gain"], layer, BF16, normed_cols=QKV_NORMED_COLS)
    aux = _inproj(x, w["mix_norm"], w["w_aux"], w["aux_gain"], layer, F32, normed_cols=0)
    o_na = _na_attention(qkv, w["na_bias"][layer], b, s)
    q, k, v = _mla_prep(aux, rope[0], rope[1], w["cq_gain"], w["ckv_gain"], w["wuq"], w["wukv"],
                        w["q_gain"], w["k_gain"], layer, s)
    o_mla = _mla_attention(q, k, v, b, s)
    dil = [_dil_group_attention(qkv, slopes[g], g, b, s) for g in range(len(DIL_GROUPS))]
    merged = _merge(o_na, o_mla, [d[0] for d in dil], [d[1] for d in dil], aux,
                    w["w_na"], w["w_mla"], w["w_dil"], layer)
    x = _resid_matmul(merged, w["w_o"], x, layer)
    return _ffn(x, *w["ffn2"], layer)


def _trunk(x, w, slopes):
    b, s, d = x.shape
    rope = _rope_tables(s)
    y = x.reshape(b * s, d)
    for layer in range(w["w_qkv"].shape[0]):
        y = _layer(y, w, layer, b, s, rope, slopes)
    return y.reshape(b, s, d)


def kernel(x_prompt, x_sample, ffn1_norm, ffn1_w_gate, ffn1_w_up, ffn1_w_down, mix_norm, w_in, na_q_norm, na_k_norm, na_rpb, mla_cq_norm, mla_ckv_norm, mla_w_uq, mla_w_ukv, mla_q_norm, mla_k_norm, dil_q_norm, dil_k_norm, w_na_out, w_mla_out, w_dil_out, w_o, ffn2_norm, ffn2_w_gate, ffn2_w_up, ffn2_w_down):
    w = _prepare(dict(
        ffn1_norm=ffn1_norm, ffn1_w_gate=ffn1_w_gate, ffn1_w_up=ffn1_w_up, ffn1_w_down=ffn1_w_down,
        mix_norm=mix_norm, w_in=w_in, na_q_norm=na_q_norm, na_k_norm=na_k_norm, na_rpb=na_rpb,
        mla_cq_norm=mla_cq_norm, mla_ckv_norm=mla_ckv_norm, mla_w_uq=mla_w_uq, mla_w_ukv=mla_w_ukv,
        mla_q_norm=mla_q_norm, mla_k_norm=mla_k_norm, dil_q_norm=dil_q_norm, dil_k_norm=dil_k_norm,
        w_na_out=w_na_out, w_mla_out=w_mla_out, w_dil_out=w_dil_out, w_o=w_o,
        ffn2_norm=ffn2_norm, ffn2_w_gate=ffn2_w_gate, ffn2_w_up=ffn2_w_up, ffn2_w_down=ffn2_w_down))
    slopes = _dil_slopes()
    return (_trunk(x_prompt, w, slopes), _trunk(x_sample, w, slopes))
```

```python
import functools

import numpy as np
import jax
import jax.numpy as jnp
from jax import lax
from jax.experimental import pallas as pl
from jax.experimental.pallas import tpu as pltpu

F32 = jnp.float32
BF16 = jnp.bfloat16

D_MODEL = 2048
GRID_W = 64
HEAD_DIM = 128
EPS = 1e-6
NEG_INF = -1e30
LOG2E = 1.4426950408889634

NA_HEADS = 8
NA_WIN_R = 8
NA_WIN_C = 16
NA_WIDTH = NA_HEADS * HEAD_DIM

MLA_HEADS = 8
MLA_Q_RANK = 512
MLA_KV_RANK = 256
MLA_NOPE = 128
MLA_ROPE = 64
MLA_V = 128
MLA_QK = MLA_NOPE + MLA_ROPE
MLA_OUT = MLA_HEADS * MLA_V
ROPE_THETA = 10000.0

DIL_GROUPS = ((128, 1), (512, 4), (2048, 16))
DIL_HEADS_PER_GROUP = 4
DIL_HEADS = 12
DIL_WIDTH = DIL_HEADS * HEAD_DIM
DIL_OUT = DIL_HEADS_PER_GROUP * HEAD_DIM

D_FF = 5632

LANES = 128
VMEM_LIMIT = 56 * 1024 * 1024

IN_TM = 1024
IN_TN = 512
QKV_COLS = 3 * NA_WIDTH + 3 * DIL_WIDTH
QKV_TILES = QKV_COLS // IN_TN
NA_TILES = 3 * NA_WIDTH // IN_TN
LAT_COLS = MLA_Q_RANK + MLA_KV_RANK + 2 * LANES
AUX_COLS = LAT_COLS + 3 * D_MODEL
AUX_TILES = AUX_COLS // IN_TN
MLA_QPAD = 2 * LANES
MLA_QEXT = 3 * LANES


def _params(*sem):
    return pltpu.CompilerParams(dimension_semantics=sem, vmem_limit_bytes=VMEM_LIMIT)


def _rms(x, gain):
    return x * lax.rsqrt(jnp.mean(x * x, axis=-1, keepdims=True) + EPS) * gain


def _dot(a, b):
    return jnp.dot(a, b, preferred_element_type=F32)


def _dot_t(a, b):
    return lax.dot_general(a, b, (((1,), (1,)), ((), ())), preferred_element_type=F32)


def _ffn_kernel(x_ref, g_ref, wg_ref, wu_ref, wd_ref, o_ref, xn_ref):
    f = pl.program_id(1)

    @pl.when(f == 0)
    def _():
        x = x_ref[...]
        xn_ref[...] = _rms(x, g_ref[...]).astype(BF16)
        o_ref[...] = x

    xn = xn_ref[...]
    g = _dot(xn, wg_ref[...])
    u = _dot(xn, wu_ref[...])
    h = (g * jax.nn.sigmoid(g) * u * 0.5).astype(BF16)
    o_ref[...] += _dot(h, wd_ref[...])


def _ffn(x, gain, wg, wu, wd, layer, *, tm=512, tf=512):
    t, d = x.shape
    ff = wg.shape[-1]
    return pl.pallas_call(
        _ffn_kernel,
        grid=(t // tm, ff // tf),
        in_specs=[
            pl.BlockSpec((tm, d), lambda i, f: (i, 0)),
            pl.BlockSpec((None, 1, d), lambda i, f: (layer, 0, 0)),
            pl.BlockSpec((None, d, tf), lambda i, f: (layer, 0, f)),
            pl.BlockSpec((None, d, tf), lambda i, f: (layer, 0, f)),
            pl.BlockSpec((None, tf, d), lambda i, f: (layer, f, 0)),
        ],
        out_specs=pl.BlockSpec((tm, d), lambda i, f: (i, 0)),
        out_shape=jax.ShapeDtypeStruct((t, d), F32),
        scratch_shapes=[pltpu.VMEM((tm, d), BF16)],
        compiler_params=_params("parallel", "arbitrary"),
        name="ffn",
    )(x, gain, wg, wu, wd)


def _inproj_kernel(x_ref, g_ref, w_ref, hg_ref, qkv_ref, aux_ref, yn_ref, xn_ref):
    j = pl.program_id(1)
    tm, d = x_ref.shape

    @pl.when(j == 0)
    def _():
        x = x_ref[...]
        inv = lax.rsqrt(jnp.mean(x * x, axis=-1, keepdims=True) + EPS)
        for c in range(d // LANES):
            cs = slice(c * LANES, (c + 1) * LANES)
            y = x_ref[:, cs] * inv * g_ref[:, cs]
            yn_ref[c] = y
            xn_ref[0, :, cs] = y.astype(BF16)
        for g in (1, 2):
            dil = DIL_GROUPS[g][1]
            rows = tm // dil
            for r in range(dil):
                for c in range(d // LANES):
                    xn_ref[g, r * rows:(r + 1) * rows, c * LANES:(c + 1) * LANES] = (
                        yn_ref[c, pl.ds(r, rows, stride=dil), :].astype(BF16))

    order = jnp.where(j < NA_TILES + 3, 0, jnp.where(j < NA_TILES + 6, 1, jnp.where(j < QKV_TILES, 2, 0)))
    acc = _dot(xn_ref[order], w_ref[...])
    tn = acc.shape[1]
    is_qkv = j < QKV_TILES
    normed = (j < 4) | ((j >= NA_TILES) & is_qkv & ((j - NA_TILES) % 3 != 2))

    @pl.when(normed)
    def _():
        hg = hg_ref[...]
        for c in range(tn // HEAD_DIM):
            sl = slice(c * HEAD_DIM, (c + 1) * HEAD_DIM)
            qkv_ref[:, sl] = _rms(acc[:, sl], hg[:, sl]).astype(BF16)

    @pl.when(is_qkv & jnp.logical_not(normed))
    def _():
        qkv_ref[...] = acc.astype(BF16)

    @pl.when(jnp.logical_not(is_qkv))
    def _():
        aux_ref[...] = acc


def _inproj(x, gain, w, head_gain, layer):
    t, d = x.shape
    tm, tn = IN_TM, IN_TN
    return pl.pallas_call(
        _inproj_kernel,
        grid=(t // tm, QKV_TILES + AUX_TILES),
        in_specs=[
            pl.BlockSpec((tm, d), lambda i, j: (i, 0)),
            pl.BlockSpec((None, 1, d), lambda i, j: (layer, 0, 0)),
            pl.BlockSpec((None, d, tn), lambda i, j: (layer, 0, j)),
            pl.BlockSpec((None, 1, tn), lambda i, j: (layer, 0, jnp.minimum(j, QKV_TILES - 1))),
        ],
        out_specs=[
            pl.BlockSpec((tm, tn), lambda i, j: (i, jnp.minimum(j, QKV_TILES - 1))),
            pl.BlockSpec((tm, tn), lambda i, j: (i, jnp.maximum(j - QKV_TILES, 0))),
        ],
        out_shape=[jax.ShapeDtypeStruct((t, QKV_COLS), BF16), jax.ShapeDtypeStruct((t, AUX_COLS), F32)],
        scratch_shapes=[pltpu.VMEM((d // LANES, tm, LANES), F32), pltpu.VMEM((3, tm, d), BF16)],
        compiler_params=_params("parallel", "arbitrary"),
        name="inproj",
    )(x, gain, w, head_gain)


NA_QROWS = 4
NA_TQ = NA_QROWS * GRID_W
NA_KROWS = 3 * NA_QROWS
NA_TK = NA_KROWS * GRID_W


def _na_bias_tables(rpb):
    heads = rpb.shape[0]
    qc = np.arange(GRID_W)[:, None]
    kc = np.arange(GRID_W)[None, :]
    col_start = np.clip(qc - NA_WIN_C // 2, 0, GRID_W - NA_WIN_C)
    col_ok = (kc >= col_start) & (kc < col_start + NA_WIN_C)
    dc_idx = np.clip(kc - qc, -(NA_WIN_C - 1), NA_WIN_C - 1) + NA_WIN_C - 1
    onehot = (dc_idx[None] == np.arange(2 * NA_WIN_C - 1)[:, None, None]).astype(np.float32)
    by_col = jnp.einsum("hrd,dqk->hqrk", rpb, jnp.asarray(onehot), precision=lax.Precision.HIGHEST)
    by_col = jnp.where(jnp.asarray(col_ok)[None, :, None, :], by_col, NEG_INF)
    tables = []
    for q_rel, k0_rel in ((0, None), (NA_QROWS, 0), (2 * NA_QROWS, None)):
        rows = []
        for j in range(NA_QROWS):
            k0 = j if k0_rel == 0 else (0 if q_rel == 0 else NA_QROWS)
            dr0 = k0 - (q_rel + j) + NA_WIN_R - 1
            win = by_col[:, :, dr0:dr0 + NA_WIN_R, :]
            pad = lambda n: jnp.full((heads, GRID_W, n, GRID_W), NEG_INF, F32)
            rows.append(jnp.concatenate([pad(k0), win, pad(NA_KROWS - NA_WIN_R - k0)], axis=2))
        tables.append(jnp.stack(rows, axis=1).reshape(heads, NA_TQ, NA_TK))
    return jnp.stack(tables)


def _na_kernel(q_ref, k0_ref, k1_ref, k2_ref, v0_ref, v1_ref, v2_ref, bias_ref, o_ref):
    scale = HEAD_DIM ** -0.5
    for h in range(NA_HEADS):
        sl = slice(h * HEAD_DIM, (h + 1) * HEAD_DIM)
        q = q_ref[:, sl]
        s = jnp.concatenate([_dot_t(q, k_ref[:, sl]) for k_ref in (k0_ref, k1_ref, k2_ref)], axis=1)
        s = s * scale + bias_ref[h]
        m = jnp.max(s, axis=-1, keepdims=True)
        e = jnp.exp(s - m)
        p = (e * (1.0 / jnp.sum(e, axis=-1, keepdims=True))).astype(BF16)
        o = _dot(p[:, 0:NA_TQ], v0_ref[:, sl])
        o += _dot(p[:, NA_TQ:2 * NA_TQ], v1_ref[:, sl])
        o += _dot(p[:, 2 * NA_TQ:3 * NA_TQ], v2_ref[:, sl])
        o_ref[:, sl] = o.astype(o_ref.dtype)


def _na_attention(qkv, bias, b, s):
    nb = s // NA_TQ
    assert nb >= 3
    q_col, k_col, v_col = 0, 1, 2

    def win(i):
        return jnp.clip(i - 1, 0, nb - 3)

    def kv_spec(col, t):
        return pl.BlockSpec((NA_TQ, NA_WIDTH), lambda bi, i: (bi * nb + win(i) + t, col))

    def bias_map(bi, i):
        kind = jnp.where(i == 0, 0, jnp.where(i == nb - 1, 2, 1))
        return (kind, 0, 0, 0)

    return pl.pallas_call(
        _na_kernel,
        grid=(b, nb),
        in_specs=[pl.BlockSpec((NA_TQ, NA_WIDTH), lambda bi, i: (bi * nb + i, q_col))]
        + [kv_spec(k_col, t) for t in range(3)]
        + [kv_spec(v_col, t) for t in range(3)]
        + [pl.BlockSpec((None, NA_HEADS, NA_TQ, NA_TK), bias_map)],
        out_specs=pl.BlockSpec((NA_TQ, NA_WIDTH), lambda bi, i: (bi * nb + i, 0)),
        out_shape=jax.ShapeDtypeStruct((b * s, NA_WIDTH), BF16),
        compiler_params=_params("parallel", "arbitrary"),
        name="na_attn",
    )(qkv, qkv, qkv, qkv, qkv, qkv, qkv, bias)


def _rope_tables(s):
    half = MLA_ROPE // 2
    inv = ROPE_THETA ** (-jnp.arange(half, dtype=F32) / half)
    ang = jnp.arange(s).astype(F32)[:, None] * inv[None, :]
    cos, sin = jnp.cos(ang), jnp.sin(ang)
    zero = jnp.zeros((s, LANES - MLA_ROPE), F32)
    return (jnp.concatenate([cos, cos, zero], axis=-1), jnp.concatenate([-sin, sin, zero], axis=-1))


def _mla_prep_kernel(cq_ref, ckv_ref, kr_ref, krr_ref, cos_ref, sin_ref, cqg_ref, ckvg_ref, wuq_ref, wukv_ref,
                     qg_ref, kg_ref, q_out, k_out, v_out):
    cos = cos_ref[...]
    sin = sin_ref[...]
    qf = _dot(_rms(cq_ref[...], cqg_ref[...]).astype(BF16), wuq_ref[...])
    kvf = _dot(_rms(ckv_ref[...], ckvg_ref[...]).astype(BF16), wukv_ref[...])
    kr = kr_ref[...] * cos + krr_ref[...] * sin
    kr_ss = jnp.sum(kr * kr, axis=-1, keepdims=True)
    qg = qg_ref[...]
    kg = kg_ref[...]
    for h in range(MLA_HEADS):
        qb = h * MLA_QEXT
        qn = qf[:, qb:qb + LANES]
        qr = qf[:, qb + LANES:qb + 2 * LANES] * cos + qf[:, qb + 2 * LANES:qb + 3 * LANES] * sin
        ms = (jnp.sum(qn * qn, axis=-1, keepdims=True) + jnp.sum(qr * qr, axis=-1, keepdims=True)) * (1.0 / MLA_QK)
        r = lax.rsqrt(ms + EPS)
        ob = h * MLA_QPAD
        q_out[:, ob:ob + LANES] = (qn * r * qg[:, :LANES]).astype(BF16)
        q_out[:, ob + LANES:ob + 2 * LANES] = (qr * r * qg[:, LANES:]).astype(BF16)
        kb = h * (MLA_NOPE + MLA_V)
        kn = kvf[:, kb:kb + MLA_NOPE]
        msk = (jnp.sum(kn * kn, axis=-1, keepdims=True) + kr_ss) * (1.0 / MLA_QK)
        rk = lax.rsqrt(msk + EPS)
        k_out[:, ob:ob + LANES] = (kn * rk * kg[:, :LANES]).astype(BF16)
        k_out[:, ob + LANES:ob + 2 * LANES] = (kr * rk * kg[:, LANES:]).astype(BF16)
        v_out[:, h * MLA_V:(h + 1) * MLA_V] = kvf[:, kb + MLA_NOPE:kb + MLA_NOPE + MLA_V].astype(BF16)


def _mla_prep(aux, cos, sin, cq_gain, ckv_gain, wuq, wukv, q_gain, k_gain, layer, s, *, tm=256):
    t = aux.shape[0]
    n_pos = s // tm
    full = lambda shape: pl.BlockSpec((None,) + shape, lambda i: (layer,) + (0,) * len(shape))
    return pl.pallas_call(
        _mla_prep_kernel,
        grid=(t // tm,),
        in_specs=[
            pl.BlockSpec((tm, MLA_Q_RANK), lambda i: (i, 0)),
            pl.BlockSpec((tm, MLA_KV_RANK), lambda i: (i, MLA_Q_RANK // MLA_KV_RANK)),
            pl.BlockSpec((tm, LANES), lambda i: (i, (MLA_Q_RANK + MLA_KV_RANK) // LANES)),
            pl.BlockSpec((tm, LANES), lambda i: (i, (MLA_Q_RANK + MLA_KV_RANK) // LANES + 1)),
            pl.BlockSpec((tm, LANES), lambda i: (i % n_pos, 0)),
            pl.BlockSpec((tm, LANES), lambda i: (i % n_pos, 0)),
            full((1, MLA_Q_RANK)),
            full((1, MLA_KV_RANK)),
            full((MLA_Q_RANK, MLA_HEADS * MLA_QEXT)),
            full((MLA_KV_RANK, MLA_HEADS * (MLA_NOPE + MLA_V))),
            full((1, MLA_QPAD)),
            full((1, MLA_QPAD)),
        ],
        out_specs=[
            pl.BlockSpec((tm, MLA_HEADS * MLA_QPAD), lambda i: (i, 0)),
            pl.BlockSpec((tm, MLA_HEADS * MLA_QPAD), lambda i: (i, 0)),
            pl.BlockSpec((tm, MLA_OUT), lambda i: (i, 0)),
        ],
        out_shape=[
            jax.ShapeDtypeStruct((t, MLA_HEADS * MLA_QPAD), BF16),
            jax.ShapeDtypeStruct((t, MLA_HEADS * MLA_QPAD), BF16),
            jax.ShapeDtypeStruct((t, MLA_OUT), BF16),
        ],
        compiler_params=_params("parallel"),
        name="mla_prep",
    )(aux, aux, aux, aux, cos, sin, cq_gain, ckv_gain, wuq, wukv, q_gain, k_gain)


MLA_KCHUNK = 256


def _lane_groups(x, op):
    out = x[:, :LANES]
    for c in range(1, x.shape[1] // LANES):
        out = op(out, x[:, c * LANES:(c + 1) * LANES])
    return out


def _mla_flash_kernel(q_ref, k_ref, v_ref, o_ref, s_ref, m_ref, l_ref, acc_ref):
    ki = pl.program_id(3)
    tk = k_ref.shape[0]
    nchunk = tk // MLA_KCHUNK

    @pl.when(ki == 0)
    def _():
        m_ref[...] = jnp.full_like(m_ref, -jnp.inf)
        l_ref[...] = jnp.zeros_like(l_ref)
        acc_ref[...] = jnp.zeros_like(acc_ref)

    q = q_ref[...]
    c2 = (MLA_QK ** -0.5) * LOG2E
    mx = None
    for c in range(nchunk):
        ks = slice(c * MLA_KCHUNK, (c + 1) * MLA_KCHUNK)
        s = _dot_t(q, k_ref[ks, :]) * c2
        s_ref[:, ks] = s
        part = _lane_groups(s, jnp.maximum)
        mx = part if mx is None else jnp.maximum(mx, part)
    m_prev = m_ref[...]
    m_new = jnp.maximum(m_prev, jnp.max(mx, axis=-1, keepdims=True))
    alpha = jnp.exp2(m_prev - m_new)
    m_ref[...] = m_new
    l = alpha * l_ref[...]
    acc = alpha * acc_ref[...]
    m_wide = jnp.concatenate([m_new] * (MLA_KCHUNK // LANES), axis=1)
    for c in range(nchunk):
        ks = slice(c * MLA_KCHUNK, (c + 1) * MLA_KCHUNK)
        p = jnp.exp2(s_ref[:, ks] - m_wide)
        l += _lane_groups(p, jnp.add)
        acc += _dot(p.astype(BF16), v_ref[ks, :])
    l_ref[...] = l
    acc_ref[...] = acc

    @pl.when(ki == pl.num_programs(3) - 1)
    def _():
        o_ref[...] = (acc * (1.0 / jnp.sum(l, axis=-1, keepdims=True))).astype(o_ref.dtype)


def _mla_attention(q, k, v, b, s, *, tq=512, tk=2048):
    nq, nk = s // tq, s // tk
    return pl.pallas_call(
        _mla_flash_kernel,
        grid=(b, MLA_HEADS, nq, nk),
        in_specs=[
            pl.BlockSpec((tq, MLA_QPAD), lambda bi, h, qi, ki: (bi * nq + qi, h)),
            pl.BlockSpec((tk, MLA_QPAD), lambda bi, h, qi, ki: (bi * nk + ki, h)),
            pl.BlockSpec((tk, MLA_V), lambda bi, h, qi, ki: (bi * nk + ki, h)),
        ],
        out_specs=pl.BlockSpec((tq, MLA_V), lambda bi, h, qi, ki: (bi * nq + qi, h)),
        out_shape=jax.ShapeDtypeStruct((b * s, MLA_OUT), BF16),
        scratch_shapes=[pltpu.VMEM((tq, tk), F32), pltpu.VMEM((tq, LANES), F32), pltpu.VMEM((tq, LANES), F32),
                        pltpu.VMEM((tq, MLA_V), F32)],
        compiler_params=_params("parallel", "parallel", "parallel", "arbitrary"),
        name="mla_flash",
    )(q, k, v)


DIL_TU = 128


def _alibi_slopes(n):
    return 2.0 ** (-8.0 * jnp.arange(1, n + 1, dtype=F32) / n)


def _dil_kernel(slope_ref, q_ref, k0_ref, k1_ref, k2_ref, v0_ref, v1_ref, v2_ref, o_ref, lse_ref, *, n, radius):
    tu = DIL_TU
    i = pl.program_id(2)
    row = lax.broadcasted_iota(jnp.int32, (tu, 3 * tu), 0)
    col = lax.broadcasted_iota(jnp.int32, (tu, 3 * tu), 1)
    kpos = (i - 1) * tu + col
    dist_i = jnp.abs(row + tu - col)
    valid = (kpos >= 0) & (kpos < n) & (dist_i <= radius)
    dist = dist_i.astype(F32)
    scale = HEAD_DIM ** -0.5
    blk = lambda ref, sl: ref[..., sl].reshape(tu, HEAD_DIM)
    for h in range(DIL_HEADS_PER_GROUP):
        sl = slice(h * HEAD_DIM, (h + 1) * HEAD_DIM)
        q = blk(q_ref, sl)
        s = jnp.concatenate([_dot_t(q, blk(k_ref, sl)) for k_ref in (k0_ref, k1_ref, k2_ref)], axis=1)
        s = s * scale - slope_ref[h] * dist
        s = jnp.where(valid, s, NEG_INF)
        m = jnp.max(s, axis=-1, keepdims=True)
        p = jnp.exp(s - m)
        den = jnp.sum(p, axis=-1, keepdims=True)
        pb = p.astype(BF16)
        o = _dot(pb[:, 0:tu], blk(v0_ref, sl))
        o += _dot(pb[:, tu:2 * tu], blk(v1_ref, sl))
        o += _dot(pb[:, 2 * tu:3 * tu], blk(v2_ref, sl))
        o_ref[..., sl] = (o * (1.0 / den)).reshape(o_ref.shape[:-1] + (HEAD_DIM,))
        lse_ref[..., sl] = jnp.broadcast_to(m + jnp.log(den), (tu, HEAD_DIM)).reshape(o_ref.shape[:-1] + (HEAD_DIM,))


def _dil_group_attention(qkv, slopes, group, b, s):
    window, dil = DIL_GROUPS[group]
    radius = (window // 2) // dil
    n = s // dil
    tu = DIL_TU
    nblk = n // tu
    assert radius <= tu and s % IN_TM == 0 and n % tu == 0
    t = b * s
    rows = IN_TM // dil
    tiles = s // IN_TM
    units = QKV_COLS // DIL_OUT
    q_unit = NA_TILES + 3 * group
    if rows >= tu:
        sub = rows // tu
        block = (None, None, tu)
        index = lambda bi, r, u: (bi * tiles + u // sub, r, u % sub)
    else:
        span = tu // rows
        assert tiles % span == 0
        block = (span, None, rows)
        index = lambda bi, r, u: (bi * tiles // span + u, r, 0)

    def spec(unit, off):
        return pl.BlockSpec(block + (DIL_OUT,),
                            lambda bi, r, i: index(bi, r, jnp.clip(i + off, 0, nblk - 1)) + (unit,))

    qkv_v = qkv.reshape(t // IN_TM, dil, rows, QKV_COLS)
    kern = functools.partial(_dil_kernel, n=n, radius=radius)
    o, lse = pl.pallas_call(
        kern,
        grid=(b, dil, nblk),
        in_specs=[pl.BlockSpec(memory_space=pltpu.SMEM), spec(q_unit, 0)]
        + [spec(q_unit + 1, off) for off in (-1, 0, 1)]
        + [spec(q_unit + 2, off) for off in (-1, 0, 1)],
        out_specs=[spec(0, 0), spec(0, 0)],
        out_shape=[jax.ShapeDtypeStruct((t // IN_TM, dil, rows, DIL_OUT), F32)] * 2,
        compiler_params=_params("parallel", "parallel", "arbitrary"),
        name=f"dil_attn_g{group}",
    )(slopes, qkv_v, qkv_v, qkv_v, qkv_v, qkv_v, qkv_v, qkv_v)
    return o, lse


MERGE_TM = 512
NCH = DIL_OUT // LANES


def _merge_kernel(ona_ref, omla_ref, od0_ref, od1_ref, od2_ref, ls0_ref, ls1_ref, ls2_ref,
                  gna_ref, gmla_ref, gdil_ref, wna_ref, wmla_ref, wdil_ref, o_ref, od_ref, tok_ref):
    j = pl.program_id(1)
    tm = MERGE_TM

    @pl.when(j == 0)
    def _():
        for a, (ref, dil) in enumerate(((od1_ref, 4), (ls1_ref, 4), (od2_ref, 16), (ls2_ref, 16))):
            for r in range(dil):
                for c in range(NCH):
                    tok_ref[a, c, pl.ds(r, tm // dil, stride=dil), :] = ref[r, :, c * LANES:(c + 1) * LANES]
        for c in range(NCH):
            cs = slice(c * LANES, (c + 1) * LANES)
            od0, ls0 = od0_ref[0, :, cs], ls0_ref[0, :, cs]
            od1, ls1, od2, ls2 = tok_ref[0, c], tok_ref[1, c], tok_ref[2, c], tok_ref[3, c]
            top = jnp.maximum(jnp.maximum(ls0, ls1), ls2)
            w0, w1, w2 = jnp.exp(ls0 - top), jnp.exp(ls1 - top), jnp.exp(ls2 - top)
            inv = 1.0 / (w0 + w1 + w2)
            od = (w0 * inv) * od0 + (w1 * inv) * od1 + (w2 * inv) * od2
            od_ref[:, cs] = od.astype(BF16)

    merged = jax.nn.sigmoid(gna_ref[...]) * _dot(ona_ref[...], wna_ref[...])
    merged += jax.nn.sigmoid(gmla_ref[...]) * _dot(omla_ref[...], wmla_ref[...])
    merged += jax.nn.sigmoid(gdil_ref[...]) * _dot(od_ref[...], wdil_ref[...])
    o_ref[...] = merged.astype(o_ref.dtype)


def _merge(o_na, o_mla, o_dil, lse_dil, aux, w_na, w_mla, w_dil, layer, *, tn=512):
    t = o_na.shape[0]
    tm = MERGE_TM
    halves = IN_TM // tm
    gate0 = LAT_COLS // tn
    gstep = D_MODEL // tn
    row = lambda width: pl.BlockSpec((tm, width), lambda i, j: (i, 0))
    gate = lambda g: pl.BlockSpec((tm, tn), lambda i, j: (i, gate0 + g * gstep + j))
    wcol = lambda k: pl.BlockSpec((None, k, tn), lambda i, j: (layer, 0, j))

    def dil_spec(g):
        dil = DIL_GROUPS[g][1]
        return pl.BlockSpec((None, dil, tm // dil, DIL_OUT), lambda i, j: (i // halves, 0, i % halves, 0))

    return pl.pallas_call(
        _merge_kernel,
        grid=(t // tm, D_MODEL // tn),
        in_specs=[row(NA_WIDTH), row(MLA_OUT)] + [dil_spec(g) for g in range(3)] * 2
        + [gate(0), gate(1), gate(2)] + [wcol(NA_WIDTH), wcol(MLA_OUT), wcol(DIL_OUT)],
        out_specs=pl.BlockSpec((tm, tn), lambda i, j: (i, j)),
        out_shape=jax.ShapeDtypeStruct((t, D_MODEL), BF16),
        scratch_shapes=[pltpu.VMEM((tm, DIL_OUT), BF16), pltpu.VMEM((4, NCH, tm, LANES), F32)],
        compiler_params=_params("parallel", "arbitrary"),
        name="merge",
    )(o_na, o_mla, *o_dil, *lse_dil, aux, aux, aux, w_na, w_mla, w_dil)


def _resid_matmul_kernel(a_ref, w_ref, x_ref, o_ref):
    o_ref[...] = x_ref[...] + _dot(a_ref[...], w_ref[...])


def _resid_matmul(a, w, x, layer, *, tm=1024, tn=512):
    t, k = a.shape
    n = w.shape[-1]
    return pl.pallas_call(
        _resid_matmul_kernel,
        grid=(t // tm, n // tn),
        in_specs=[
            pl.BlockSpec((tm, k), lambda i, j: (i, 0)),
            pl.BlockSpec((None, k, tn), lambda i, j: (layer, 0, j)),
            pl.BlockSpec((tm, tn), lambda i, j: (i, j)),
        ],
        out_specs=pl.BlockSpec((tm, tn), lambda i, j: (i, j)),
        out_shape=jax.ShapeDtypeStruct((t, n), F32),
        compiler_params=_params("parallel", "arbitrary"),
        name="wo_resid",
    )(a, w, x)


def _prepare(p):
    depth = p["w_in"].shape[0]
    row = lambda g: g[:, None, :].astype(F32)
    w_in = p["w_in"]
    o = np.cumsum((0, NA_WIDTH, NA_WIDTH, NA_WIDTH, MLA_Q_RANK, MLA_KV_RANK, MLA_ROPE,
                   DIL_WIDTH, DIL_WIDTH, DIL_WIDTH, D_MODEL, D_MODEL, D_MODEL))
    seg = lambda a: w_in[:, :, o[a]:o[a + 1]]
    grp = lambda a, g: seg(a)[:, :, g * DIL_OUT:(g + 1) * DIL_OUT]
    rot = np.concatenate([np.arange(MLA_ROPE // 2, MLA_ROPE), np.arange(MLA_ROPE // 2)])
    zpad = jnp.zeros((depth, D_MODEL, LANES - MLA_ROPE), w_in.dtype)
    cols = [seg(0), seg(1), seg(2)]
    for g in range(len(DIL_GROUPS)):
        cols += [grp(6, g), grp(7, g), grp(8, g)]
    cols += [seg(3), seg(4), seg(5), zpad, seg(5)[:, :, rot], zpad, seg(9), seg(10), seg(11)]
    w_all = jnp.concatenate(cols, axis=-1).astype(BF16)

    tile = lambda g, reps: jnp.tile(g, (1, reps))
    ones = lambda n: jnp.ones((depth, n), F32)
    dil_gain = jnp.concatenate([tile(p["dil_q_norm"], DIL_HEADS_PER_GROUP), tile(p["dil_k_norm"], DIL_HEADS_PER_GROUP),
                                ones(DIL_OUT)], axis=-1)
    head_gain = jnp.concatenate(
        [tile(p["na_q_norm"], NA_HEADS), tile(p["na_k_norm"], NA_HEADS), ones(NA_WIDTH)]
        + [dil_gain] * len(DIL_GROUPS), axis=-1)

    wuq = p["mla_w_uq"].reshape(depth, MLA_Q_RANK, MLA_HEADS, MLA_QK)
    zq = jnp.zeros((depth, MLA_Q_RANK, MLA_HEADS, LANES - MLA_ROPE), wuq.dtype)
    wuq_rope = wuq[..., MLA_NOPE:]
    wuq = jnp.concatenate([wuq[..., :MLA_NOPE], wuq_rope, zq, wuq_rope[..., rot], zq], axis=-1)
    pad_gain = lambda g: jnp.concatenate([g, jnp.zeros((depth, MLA_QPAD - MLA_QK), F32)], axis=-1)

    return dict(
        ffn1=(row(p["ffn1_norm"]), p["ffn1_w_gate"].astype(BF16), p["ffn1_w_up"].astype(BF16),
              p["ffn1_w_down"].astype(BF16)),
        ffn2=(row(p["ffn2_norm"]), p["ffn2_w_gate"].astype(BF16), p["ffn2_w_up"].astype(BF16),
              p["ffn2_w_down"].astype(BF16)),
        mix_norm=row(p["mix_norm"]),
        w_all=w_all, head_gain=row(head_gain),
        na_bias=[_na_bias_tables(p["na_rpb"][l]) for l in range(depth)],
        cq_gain=row(p["mla_cq_norm"]), ckv_gain=row(p["mla_ckv_norm"]),
        wuq=wuq.reshape(depth, MLA_Q_RANK, MLA_HEADS * MLA_QEXT).astype(BF16),
        wukv=p["mla_w_ukv"].astype(BF16),
        q_gain=row(pad_gain(p["mla_q_norm"])), k_gain=row(pad_gain(p["mla_k_norm"])),
        w_na=p["w_na_out"].astype(BF16), w_mla=p["w_mla_out"].astype(BF16), w_dil=p["w_dil_out"].astype(BF16),
        w_o=p["w_o"].astype(BF16),
    )


def _dil_slopes():
    slopes = _alibi_slopes(DIL_HEADS)
    out = []
    for g, (_, dil) in enumerate(DIL_GROUPS):
        lo, hi = g * DIL_HEADS_PER_GROUP, (g + 1) * DIL_HEADS_PER_GROUP
        out.append(slopes[lo:hi] * dil)
    return out


def _layer(x, w, layer, b, s, rope, slopes):
    x = _ffn(x, *w["ffn1"], layer)
    qkv, aux = _inproj(x, w["mix_norm"], w["w_all"], w["head_gain"], layer)
    o_na = _na_attention(qkv, w["na_bias"][layer], b, s)
    q, k, v = _mla_prep(aux, rope[0], rope[1], w["cq_gain"], w["ckv_gain"], w["wuq"], w["wukv"],
                        w["q_gain"], w["k_gain"], layer, s)
    o_mla = _mla_attention(q, k, v, b, s)
    dil = [_dil_group_attention(qkv, slopes[g], g, b, s) for g in range(len(DIL_GROUPS))]
    merged = _merge(o_na, o_mla, [d[0] for d in dil], [d[1] for d in dil], aux,
                    w["w_na"], w["w_mla"], w["w_dil"], layer)
    x = _resid_matmul(merged, w["w_o"], x, layer)
    return _ffn(x, *w["ffn2"], layer)


def _trunk(x, w, slopes):
    b, s, d = x.shape
    rope = _rope_tables(s)
    y = x.reshape(b * s, d)
    for layer in range(w["w_all"].shape[0]):
        y = _layer(y, w, layer, b, s, rope, slopes)
    return y.reshape(b, s, d)


def kernel(x_prompt, x_sample, ffn1_norm, ffn1_w_gate, ffn1_w_up, ffn1_w_down, mix_norm, w_in, na_q_norm, na_k_norm, na_rpb, mla_cq_norm, mla_ckv_norm, mla_w_uq, mla_w_ukv, mla_q_norm, mla_k_norm, dil_q_norm, dil_k_norm, w_na_out, w_mla_out, w_dil_out, w_o, ffn2_norm, ffn2_w_gate, ffn2_w_up, ffn2_w_down):
    w = _prepare(dict(
        ffn1_norm=ffn1_norm, ffn1_w_gate=ffn1_w_gate, ffn1_w_up=ffn1_w_up, ffn1_w_down=ffn1_w_down,
        mix_norm=mix_norm, w_in=w_in, na_q_norm=na_q_norm, na_k_norm=na_k_norm, na_rpb=na_rpb,
        mla_cq_norm=mla_cq_norm, mla_ckv_norm=mla_ckv_norm, mla_w_uq=mla_w_uq, mla_w_ukv=mla_w_ukv,
        mla_q_norm=mla_q_norm, mla_k_norm=mla_k_norm, dil_q_norm=dil_q_norm, dil_k_norm=dil_k_norm,
        w_na_out=w_na_out, w_mla_out=w_mla_out, w_dil_out=w_dil_out, w_o=w_o,
        ffn2_norm=ffn2_norm, ffn2_w_gate=ffn2_w_gate, ffn2_w_up=ffn2_w_up, ffn2_w_down=ffn2_w_down))
    slopes = _dil_slopes()
    return (_trunk(x_prompt, w, slopes), _trunk(x_sample, w, slopes))
```

```python
import functools

import numpy as np
import jax
import jax.numpy as jnp
from jax import lax
from jax.experimental import pallas as pl
from jax.experimental.pallas import tpu as pltpu

F32 = jnp.float32
BF16 = jnp.bfloat16

D_MODEL = 2048
GRID_W = 64
HEAD_DIM = 128
EPS = 1e-6
NEG_INF = -1e30
LOG2E = 1.4426950408889634

NA_HEADS = 8
NA_WIN_R = 8
NA_WIN_C = 16
NA_WIDTH = NA_HEADS * HEAD_DIM

MLA_HEADS = 8
MLA_Q_RANK = 512
MLA_KV_RANK = 256
MLA_NOPE = 128
MLA_ROPE = 64
MLA_V = 128
MLA_QK = MLA_NOPE + MLA_ROPE
MLA_OUT = MLA_HEADS * MLA_V
ROPE_THETA = 10000.0

DIL_GROUPS = ((128, 1), (512, 4), (2048, 16))
DIL_HEADS_PER_GROUP = 4
DIL_HEADS = 12
DIL_WIDTH = DIL_HEADS * HEAD_DIM
DIL_OUT = DIL_HEADS_PER_GROUP * HEAD_DIM

D_FF = 5632

LANES = 128
VMEM_LIMIT = 56 * 1024 * 1024

IN_TM = 1024
IN_TN = 512
QKV_COLS = 3 * NA_WIDTH + 3 * DIL_WIDTH
QKV_TILES = QKV_COLS // IN_TN
NA_TILES = 3 * NA_WIDTH // IN_TN
LAT_COLS = MLA_Q_RANK + MLA_KV_RANK + 2 * LANES
AUX_COLS = LAT_COLS + 3 * D_MODEL
AUX_TILES = AUX_COLS // IN_TN
MLA_QPAD = 2 * LANES
MLA_QEXT = 3 * LANES


def _params(*sem):
    return pltpu.CompilerParams(dimension_semantics=sem, vmem_limit_bytes=VMEM_LIMIT)


def _rms(x, gain):
    return x * lax.rsqrt(jnp.mean(x * x, axis=-1, keepdims=True) + EPS) * gain


def _dot(a, b):
    return jnp.dot(a, b, preferred_element_type=F32)


def _dot_t(a, b):
    return lax.dot_general(a, b, (((1,), (1,)), ((), ())), preferred_element_type=F32)


def _ffn_kernel(x_ref, g_ref, wg_ref, wu_ref, wd_ref, o_ref, xn_ref):
    f = pl.program_id(1)

    @pl.when(f == 0)
    def _():
        x = x_ref[...]
        xn_ref[...] = _rms(x, g_ref[...]).astype(BF16)
        o_ref[...] = x

    xn = xn_ref[...]
    g = _dot(xn, wg_ref[...])
    u = _dot(xn, wu_ref[...])
    h = (g * jax.nn.sigmoid(g) * u * 0.5).astype(BF16)
    o_ref[...] += _dot(h, wd_ref[...])


def _ffn(x, gain, wg, wu, wd, layer, *, tm=1024, tf=512):
    t, d = x.shape
    ff = wg.shape[-1]
    return pl.pallas_call(
        _ffn_kernel,
        grid=(t // tm, ff // tf),
        in_specs=[
            pl.BlockSpec((tm, d), lambda i, f: (i, 0)),
            pl.BlockSpec((None, 1, d), lambda i, f: (layer, 0, 0)),
            pl.BlockSpec((None, d, tf), lambda i, f: (layer, 0, f)),
            pl.BlockSpec((None, d, tf), lambda i, f: (layer, 0, f)),
            pl.BlockSpec((None, tf, d), lambda i, f: (layer, f, 0)),
        ],
        out_specs=pl.BlockSpec((tm, d), lambda i, f: (i, 0)),
        out_shape=jax.ShapeDtypeStruct((t, d), F32),
        scratch_shapes=[pltpu.VMEM((tm, d), BF16)],
        compiler_params=_params("parallel", "arbitrary"),
        name="ffn",
    )(x, gain, wg, wu, wd)


def _inproj_kernel(x_ref, g_ref, w_ref, hg_ref, qkv_ref, aux_ref, yn_ref, xn_ref):
    j = pl.program_id(1)
    tm, d = x_ref.shape

    @pl.when(j == 0)
    def _():
        x = x_ref[...]
        inv = lax.rsqrt(jnp.mean(x * x, axis=-1, keepdims=True) + EPS)
        for c in range(d // LANES):
            cs = slice(c * LANES, (c + 1) * LANES)
            y = x_ref[:, cs] * inv * g_ref[:, cs]
            yn_ref[c] = y
            xn_ref[0, :, cs] = y.astype(BF16)
        for g in (1, 2):
            dil = DIL_GROUPS[g][1]
            rows = tm // dil
            for r in range(dil):
                for c in range(d // LANES):
                    xn_ref[g, r * rows:(r + 1) * rows, c * LANES:(c + 1) * LANES] = (
                        yn_ref[c, pl.ds(r, rows, stride=dil), :].astype(BF16))

    order = jnp.where(j < NA_TILES + 3, 0, jnp.where(j < NA_TILES + 6, 1, jnp.where(j < QKV_TILES, 2, 0)))
    acc = _dot(xn_ref[order], w_ref[...])
    tn = acc.shape[1]
    is_qkv = j < QKV_TILES
    normed = (j < 4) | ((j >= NA_TILES) & is_qkv & ((j - NA_TILES) % 3 != 2))

    @pl.when(normed)
    def _():
        hg = hg_ref[...]
        for c in range(tn // HEAD_DIM):
            sl = slice(c * HEAD_DIM, (c + 1) * HEAD_DIM)
            qkv_ref[:, sl] = _rms(acc[:, sl], hg[:, sl]).astype(BF16)

    @pl.when(is_qkv & jnp.logical_not(normed))
    def _():
        qkv_ref[...] = acc.astype(BF16)

    @pl.when(jnp.logical_not(is_qkv))
    def _():
        aux_ref[...] = acc


def _inproj(x, gain, w, head_gain, layer):
    t, d = x.shape
    tm, tn = IN_TM, IN_TN
    return pl.pallas_call(
        _inproj_kernel,
        grid=(t // tm, QKV_TILES + AUX_TILES),
        in_specs=[
            pl.BlockSpec((tm, d), lambda i, j: (i, 0)),
            pl.BlockSpec((None, 1, d), lambda i, j: (layer, 0, 0)),
            pl.BlockSpec((None, d, tn), lambda i, j: (layer, 0, j)),
            pl.BlockSpec((None, 1, tn), lambda i, j: (layer, 0, jnp.minimum(j, QKV_TILES - 1))),
        ],
        out_specs=[
            pl.BlockSpec((tm, tn), lambda i, j: (i, jnp.minimum(j, QKV_TILES - 1))),
            pl.BlockSpec((tm, tn), lambda i, j: (i, jnp.maximum(j - QKV_TILES, 0))),
        ],
        out_shape=[jax.ShapeDtypeStruct((t, QKV_COLS), BF16), jax.ShapeDtypeStruct((t, AUX_COLS), F32)],
        scratch_shapes=[pltpu.VMEM((d // LANES, tm, LANES), F32), pltpu.VMEM((3, tm, d), BF16)],
        compiler_params=_params("parallel", "arbitrary"),
        name="inproj",
    )(x, gain, w, head_gain)


NA_QROWS = 4
NA_TQ = NA_QROWS * GRID_W
NA_KROWS = 3 * NA_QROWS
NA_TK = NA_KROWS * GRID_W


def _na_bias_tables(rpb):
    heads = rpb.shape[0]
    qc = np.arange(GRID_W)[:, None]
    kc = np.arange(GRID_W)[None, :]
    col_start = np.clip(qc - NA_WIN_C // 2, 0, GRID_W - NA_WIN_C)
    col_ok = (kc >= col_start) & (kc < col_start + NA_WIN_C)
    dc_idx = np.clip(kc - qc, -(NA_WIN_C - 1), NA_WIN_C - 1) + NA_WIN_C - 1
    onehot = (dc_idx[None] == np.arange(2 * NA_WIN_C - 1)[:, None, None]).astype(np.float32)
    by_col = jnp.einsum("hrd,dqk->hqrk", rpb, jnp.asarray(onehot), precision=lax.Precision.HIGHEST)
    by_col = jnp.where(jnp.asarray(col_ok)[None, :, None, :], by_col, NEG_INF)
    tables = []
    for q_rel, k0_rel in ((0, None), (NA_QROWS, 0), (2 * NA_QROWS, None)):
        rows = []
        for j in range(NA_QROWS):
            k0 = j if k0_rel == 0 else (0 if q_rel == 0 else NA_QROWS)
            dr0 = k0 - (q_rel + j) + NA_WIN_R - 1
            win = by_col[:, :, dr0:dr0 + NA_WIN_R, :]
            pad = lambda n: jnp.full((heads, GRID_W, n, GRID_W), NEG_INF, F32)
            rows.append(jnp.concatenate([pad(k0), win, pad(NA_KROWS - NA_WIN_R - k0)], axis=2))
        tables.append(jnp.stack(rows, axis=1).reshape(heads, NA_TQ, NA_TK))
    return jnp.stack(tables)


def _na_kernel(q_ref, k0_ref, k1_ref, k2_ref, v0_ref, v1_ref, v2_ref, bias_ref, o_ref):
    scale = HEAD_DIM ** -0.5
    for h in range(NA_HEADS):
        sl = slice(h * HEAD_DIM, (h + 1) * HEAD_DIM)
        q = q_ref[:, sl]
        s = jnp.concatenate([_dot_t(q, k_ref[:, sl]) for k_ref in (k0_ref, k1_ref, k2_ref)], axis=1)
        s = s * scale + bias_ref[h]
        m = jnp.max(s, axis=-1, keepdims=True)
        e = jnp.exp(s - m)
        p = (e * (1.0 / jnp.sum(e, axis=-1, keepdims=True))).astype(BF16)
        o = _dot(p[:, 0:NA_TQ], v0_ref[:, sl])
        o += _dot(p[:, NA_TQ:2 * NA_TQ], v1_ref[:, sl])
        o += _dot(p[:, 2 * NA_TQ:3 * NA_TQ], v2_ref[:, sl])
        o_ref[:, sl] = o.astype(o_ref.dtype)


def _na_attention(qkv, bias, b, s):
    nb = s // NA_TQ
    assert nb >= 3
    q_col, k_col, v_col = 0, 1, 2

    def win(i):
        return jnp.clip(i - 1, 0, nb - 3)

    def kv_spec(col, t):
        return pl.BlockSpec((NA_TQ, NA_WIDTH), lambda bi, i: (bi * nb + win(i) + t, col))

    def bias_map(bi, i):
        kind = jnp.where(i == 0, 0, jnp.where(i == nb - 1, 2, 1))
        return (kind, 0, 0, 0)

    return pl.pallas_call(
        _na_kernel,
        grid=(b, nb),
        in_specs=[pl.BlockSpec((NA_TQ, NA_WIDTH), lambda bi, i: (bi * nb + i, q_col))]
        + [kv_spec(k_col, t) for t in range(3)]
        + [kv_spec(v_col, t) for t in range(3)]
        + [pl.BlockSpec((None, NA_HEADS, NA_TQ, NA_TK), bias_map)],
        out_specs=pl.BlockSpec((NA_TQ, NA_WIDTH), lambda bi, i: (bi * nb + i, 0)),
        out_shape=jax.ShapeDtypeStruct((b * s, NA_WIDTH), BF16),
        compiler_params=_params("parallel", "arbitrary"),
        name="na_attn",
    )(qkv, qkv, qkv, qkv, qkv, qkv, qkv, bias)


def _rope_tables(s):
    half = MLA_ROPE // 2
    inv = ROPE_THETA ** (-jnp.arange(half, dtype=F32) / half)
    ang = jnp.arange(s).astype(F32)[:, None] * inv[None, :]
    cos, sin = jnp.cos(ang), jnp.sin(ang)
    zero = jnp.zeros((s, LANES - MLA_ROPE), F32)
    return (jnp.concatenate([cos, cos, zero], axis=-1), jnp.concatenate([-sin, sin, zero], axis=-1))


def _mla_prep_kernel(cq_ref, ckv_ref, kr_ref, krr_ref, cos_ref, sin_ref, cqg_ref, ckvg_ref, wuq_ref, wukv_ref,
                     qg_ref, kg_ref, q_out, k_out, v_out):
    cos = cos_ref[...]
    sin = sin_ref[...]
    qf = _dot(_rms(cq_ref[...], cqg_ref[...]).astype(BF16), wuq_ref[...])
    kvf = _dot(_rms(ckv_ref[...], ckvg_ref[...]).astype(BF16), wukv_ref[...])
    kr = kr_ref[...] * cos + krr_ref[...] * sin
    kr_ss = jnp.sum(kr * kr, axis=-1, keepdims=True)
    qg = qg_ref[...]
    kg = kg_ref[...]
    for h in range(MLA_HEADS):
        qb = h * MLA_QEXT
        qn = qf[:, qb:qb + LANES]
        qr = qf[:, qb + LANES:qb + 2 * LANES] * cos + qf[:, qb + 2 * LANES:qb + 3 * LANES] * sin
        ms = (jnp.sum(qn * qn, axis=-1, keepdims=True) + jnp.sum(qr * qr, axis=-1, keepdims=True)) * (1.0 / MLA_QK)
        r = lax.rsqrt(ms + EPS)
        ob = h * MLA_QPAD
        q_out[:, ob:ob + LANES] = (qn * r * qg[:, :LANES]).astype(BF16)
        q_out[:, ob + LANES:ob + 2 * LANES] = (qr * r * qg[:, LANES:]).astype(BF16)
        kb = h * (MLA_NOPE + MLA_V)
        kn = kvf[:, kb:kb + MLA_NOPE]
        msk = (jnp.sum(kn * kn, axis=-1, keepdims=True) + kr_ss) * (1.0 / MLA_QK)
        rk = lax.rsqrt(msk + EPS)
        k_out[:, ob:ob + LANES] = (kn * rk * kg[:, :LANES]).astype(BF16)
        k_out[:, ob + LANES:ob + 2 * LANES] = (kr * rk * kg[:, LANES:]).astype(BF16)
        v_out[:, h * MLA_V:(h + 1) * MLA_V] = kvf[:, kb + MLA_NOPE:kb + MLA_NOPE + MLA_V].astype(BF16)


def _mla_prep(aux, cos, sin, cq_gain, ckv_gain, wuq, wukv, q_gain, k_gain, layer, s, *, tm=256):
    t = aux.shape[0]
    n_pos = s // tm
    full = lambda shape: pl.BlockSpec((None,) + shape, lambda i: (layer,) + (0,) * len(shape))
    return pl.pallas_call(
        _mla_prep_kernel,
        grid=(t // tm,),
        in_specs=[
            pl.BlockSpec((tm, MLA_Q_RANK), lambda i: (i, 0)),
            pl.BlockSpec((tm, MLA_KV_RANK), lambda i: (i, MLA_Q_RANK // MLA_KV_RANK)),
            pl.BlockSpec((tm, LANES), lambda i: (i, (MLA_Q_RANK + MLA_KV_RANK) // LANES)),
            pl.BlockSpec((tm, LANES), lambda i: (i, (MLA_Q_RANK + MLA_KV_RANK) // LANES + 1)),
            pl.BlockSpec((tm, LANES), lambda i: (i % n_pos, 0)),
            pl.BlockSpec((tm, LANES), lambda i: (i % n_pos, 0)),
            full((1, MLA_Q_RANK)),
            full((1, MLA_KV_RANK)),
            full((MLA_Q_RANK, MLA_HEADS * MLA_QEXT)),
            full((MLA_KV_RANK, MLA_HEADS * (MLA_NOPE + MLA_V))),
            full((1, MLA_QPAD)),
            full((1, MLA_QPAD)),
        ],
        out_specs=[
            pl.BlockSpec((tm, MLA_HEADS * MLA_QPAD), lambda i: (i, 0)),
            pl.BlockSpec((tm, MLA_HEADS * MLA_QPAD), lambda i: (i, 0)),
            pl.BlockSpec((tm, MLA_OUT), lambda i: (i, 0)),
        ],
        out_shape=[
            jax.ShapeDtypeStruct((t, MLA_HEADS * MLA_QPAD), BF16),
            jax.ShapeDtypeStruct((t, MLA_HEADS * MLA_QPAD), BF16),
            jax.ShapeDtypeStruct((t, MLA_OUT), BF16),
        ],
        compiler_params=_params("parallel"),
        name="mla_prep",
    )(aux, aux, aux, aux, cos, sin, cq_gain, ckv_gain, wuq, wukv, q_gain, k_gain)


MLA_TQ = 1024
MLA_KCHUNK = 512


def _lane_groups(x, op):
    out = x[:, :LANES]
    for c in range(1, x.shape[1] // LANES):
        out = op(out, x[:, c * LANES:(c + 1) * LANES])
    return out


def _mla_flash_kernel(q_ref, k_ref, v_ref, o_ref, m_ref, l_ref, acc_ref):
    ki = pl.program_id(3)

    @pl.when(ki == 0)
    def _():
        m_ref[...] = jnp.full_like(m_ref, -jnp.inf)
        l_ref[...] = jnp.zeros_like(l_ref)
        acc_ref[...] = jnp.zeros_like(acc_ref)

    q = q_ref[...]
    c2 = (MLA_QK ** -0.5) * LOG2E
    m, l, acc = m_ref[...], l_ref[...], acc_ref[...]
    for c in range(k_ref.shape[0] // MLA_KCHUNK):
        ks = slice(c * MLA_KCHUNK, (c + 1) * MLA_KCHUNK)
        s = _dot_t(q, k_ref[ks, :]) * c2
        m_new = jnp.maximum(m, jnp.max(_lane_groups(s, jnp.maximum), axis=-1, keepdims=True))
        alpha = jnp.exp2(m - m_new)
        p = jnp.exp2(s - jnp.concatenate([m_new] * (MLA_KCHUNK // LANES), axis=1))
        l = alpha * l + _lane_groups(p, jnp.add)
        acc = alpha * acc + _dot(p.astype(BF16), v_ref[ks, :])
        m = m_new
    m_ref[...] = m
    l_ref[...] = l
    acc_ref[...] = acc

    @pl.when(ki == pl.num_programs(3) - 1)
    def _():
        o_ref[...] = (acc * (1.0 / jnp.sum(l, axis=-1, keepdims=True))).astype(o_ref.dtype)


def _mla_attention(q, k, v, b, s, *, tk=8192):
    tq, tk = MLA_TQ, min(tk, s)
    nq, nk = s // tq, s // tk
    stat = pltpu.VMEM((tq, LANES), F32)
    return pl.pallas_call(
        _mla_flash_kernel,
        grid=(b, MLA_HEADS, nq, nk),
        in_specs=[
            pl.BlockSpec((tq, MLA_QPAD), lambda bi, h, qi, ki: (bi * nq + qi, h)),
            pl.BlockSpec((tk, MLA_QPAD), lambda bi, h, qi, ki: (bi * nk + ki, h)),
            pl.BlockSpec((tk, MLA_V), lambda bi, h, qi, ki: (bi * nk + ki, h)),
        ],
        out_specs=pl.BlockSpec((tq, MLA_V), lambda bi, h, qi, ki: (bi * nq + qi, h)),
        out_shape=jax.ShapeDtypeStruct((b * s, MLA_OUT), BF16),
        scratch_shapes=[stat, stat, pltpu.VMEM((tq, MLA_V), F32)],
        compiler_params=_params("parallel", "parallel", "parallel", "arbitrary"),
        name="mla_flash",
    )(q, k, v)


DIL_TQ = 256
DIL_KBLOCKS = 4


def _alibi_slopes(n):
    return 2.0 ** (-8.0 * jnp.arange(1, n + 1, dtype=F32) / n)


def _dil_kernel(slope_ref, q_ref, *refs, n, radius, tq):
    k_refs, v_refs = refs[:DIL_KBLOCKS], refs[DIL_KBLOCKS:2 * DIL_KBLOCKS]
    o_ref, lse_ref = refs[2 * DIL_KBLOCKS:]
    kb = tq // 2
    tk = DIL_KBLOCKS * kb
    i = pl.program_id(2)
    row = lax.broadcasted_iota(jnp.int32, (tq, tk), 0)
    col = lax.broadcasted_iota(jnp.int32, (tq, tk), 1)
    kpos = (2 * i - 1) * kb + col
    dist_i = jnp.abs(row + kb - col)
    valid = (kpos >= 0) & (kpos < n) & (dist_i <= radius)
    dist = dist_i.astype(F32)
    scale = HEAD_DIM ** -0.5
    for h in range(DIL_HEADS_PER_GROUP):
        sl = slice(h * HEAD_DIM, (h + 1) * HEAD_DIM)
        q = q_ref[..., sl].reshape(tq, HEAD_DIM)
        k = jnp.concatenate([ref[..., sl].reshape(kb, HEAD_DIM) for ref in k_refs], axis=0)
        v = jnp.concatenate([ref[..., sl].reshape(kb, HEAD_DIM) for ref in v_refs], axis=0)
        s = _dot_t(q, k) * scale - slope_ref[h] * dist
        s = jnp.where(valid, s, NEG_INF)
        m = jnp.max(s, axis=-1, keepdims=True)
        p = jnp.exp(s - m)
        den = jnp.sum(p, axis=-1, keepdims=True)
        o = _dot(p.astype(BF16), v) * (1.0 / den)
        o_ref[..., sl] = o.reshape(o_ref.shape[:-1] + (HEAD_DIM,))
        lse_ref[..., sl] = jnp.broadcast_to(m + jnp.log(den), (tq, HEAD_DIM)).reshape(o_ref.shape[:-1] + (HEAD_DIM,))


def _dil_group_attention(qkv, slopes, group, b, s):
    window, dil = DIL_GROUPS[group]
    radius = (window // 2) // dil
    n = s // dil
    tq = min(DIL_TQ, n)
    kb = tq // 2
    nblk = n // tq
    assert radius <= kb and s % IN_TM == 0 and n % tq == 0
    t = b * s
    rows = IN_TM // dil
    tiles = s // IN_TM
    q_unit = NA_TILES + 3 * group

    def spec(br, unit, pick):
        if rows >= br:
            sub = rows // br
            block = (None, None, br, DIL_OUT)
            index = lambda bi, r, u: (bi * tiles + u // sub, r, u % sub, unit)
        else:
            span = br // rows
            assert tiles % span == 0
            block = (span, None, rows, DIL_OUT)
            index = lambda bi, r, u: (bi * tiles // span + u, r, 0, unit)
        return pl.BlockSpec(block, lambda bi, r, i: index(bi, r, pick(i)))

    def kv_specs(unit):
        return [spec(kb, unit, lambda i, j=j: jnp.clip(2 * i - 1 + j, 0, n // kb - 1)) for j in range(DIL_KBLOCKS)]

    qkv_v = qkv.reshape(t // IN_TM, dil, rows, QKV_COLS)
    kern = functools.partial(_dil_kernel, n=n, radius=radius, tq=tq)
    out_spec = spec(tq, 0, lambda i: i)
    o, lse = pl.pallas_call(
        kern,
        grid=(b, dil, nblk),
        in_specs=[pl.BlockSpec(memory_space=pltpu.SMEM), spec(tq, q_unit, lambda i: i)]
        + kv_specs(q_unit + 1) + kv_specs(q_unit + 2),
        out_specs=[out_spec, out_spec],
        out_shape=[jax.ShapeDtypeStruct((t // IN_TM, dil, rows, DIL_OUT), F32)] * 2,
        compiler_params=_params("parallel", "parallel", "arbitrary"),
        name=f"dil_attn_g{group}",
    )(slopes, *([qkv_v] * (1 + 2 * DIL_KBLOCKS)))
    return o, lse


MERGE_TM = 512
NCH = DIL_OUT // LANES


def _merge_kernel(ona_ref, omla_ref, od0_ref, od1_ref, od2_ref, ls0_ref, ls1_ref, ls2_ref,
                  gna_ref, gmla_ref, gdil_ref, wna_ref, wmla_ref, wdil_ref, o_ref, od_ref, tok_ref):
    j = pl.program_id(1)
    tm = MERGE_TM

    @pl.when(j == 0)
    def _():
        for a, (ref, dil) in enumerate(((od1_ref, 4), (ls1_ref, 4), (od2_ref, 16), (ls2_ref, 16))):
            for r in range(dil):
                for c in range(NCH):
                    tok_ref[a, c, pl.ds(r, tm // dil, stride=dil), :] = ref[r, :, c * LANES:(c + 1) * LANES]
        for c in range(NCH):
            cs = slice(c * LANES, (c + 1) * LANES)
            od0, ls0 = od0_ref[0, :, cs], ls0_ref[0, :, cs]
            od1, ls1, od2, ls2 = tok_ref[0, c], tok_ref[1, c], tok_ref[2, c], tok_ref[3, c]
            top = jnp.maximum(jnp.maximum(ls0, ls1), ls2)
            w0, w1, w2 = jnp.exp(ls0 - top), jnp.exp(ls1 - top), jnp.exp(ls2 - top)
            inv = 1.0 / (w0 + w1 + w2)
            od = (w0 * inv) * od0 + (w1 * inv) * od1 + (w2 * inv) * od2
            od_ref[:, cs] = od.astype(BF16)

    merged = jax.nn.sigmoid(gna_ref[...]) * _dot(ona_ref[...], wna_ref[...])
    merged += jax.nn.sigmoid(gmla_ref[...]) * _dot(omla_ref[...], wmla_ref[...])
    merged += jax.nn.sigmoid(gdil_ref[...]) * _dot(od_ref[...], wdil_ref[...])
    o_ref[...] = merged.astype(o_ref.dtype)


def _merge(o_na, o_mla, o_dil, lse_dil, aux, w_na, w_mla, w_dil, layer, *, tn=512):
    t = o_na.shape[0]
    tm = MERGE_TM
    halves = IN_TM // tm
    gate0 = LAT_COLS // tn
    gstep = D_MODEL // tn
    row = lambda width: pl.BlockSpec((tm, width), lambda i, j: (i, 0))
    gate = lambda g: pl.BlockSpec((tm, tn), lambda i, j: (i, gate0 + g * gstep + j))
    wcol = lambda k: pl.BlockSpec((None, k, tn), lambda i, j: (layer, 0, j))

    def dil_spec(g):
        dil = DIL_GROUPS[g][1]
        return pl.BlockSpec((None, dil, tm // dil, DIL_OUT), lambda i, j: (i // halves, 0, i % halves, 0))

    return pl.pallas_call(
        _merge_kernel,
        grid=(t // tm, D_MODEL // tn),
        in_specs=[row(NA_WIDTH), row(MLA_OUT)] + [dil_spec(g) for g in range(3)] * 2
        + [gate(0), gate(1), gate(2)] + [wcol(NA_WIDTH), wcol(MLA_OUT), wcol(DIL_OUT)],
        out_specs=pl.BlockSpec((tm, tn), lambda i, j: (i, j)),
        out_shape=jax.ShapeDtypeStruct((t, D_MODEL), BF16),
        scratch_shapes=[pltpu.VMEM((tm, DIL_OUT), BF16), pltpu.VMEM((4, NCH, tm, LANES), F32)],
        compiler_params=_params("parallel", "arbitrary"),
        name="merge",
    )(o_na, o_mla, *o_dil, *lse_dil, aux, aux, aux, w_na, w_mla, w_dil)


def _resid_matmul_kernel(a_ref, w_ref, x_ref, o_ref):
    o_ref[...] = x_ref[...] + _dot(a_ref[...], w_ref[...])


def _resid_matmul(a, w, x, layer, *, tm=1024, tn=512):
    t, k = a.shape
    n = w.shape[-1]
    return pl.pallas_call(
        _resid_matmul_kernel,
        grid=(t // tm, n // tn),
        in_specs=[
            pl.BlockSpec((tm, k), lambda i, j: (i, 0)),
            pl.BlockSpec((None, k, tn), lambda i, j: (layer, 0, j)),
            pl.BlockSpec((tm, tn), lambda i, j: (i, j)),
        ],
        out_specs=pl.BlockSpec((tm, tn), lambda i, j: (i, j)),
        out_shape=jax.ShapeDtypeStruct((t, n), F32),
        compiler_params=_params("parallel", "arbitrary"),
        name="wo_resid",
    )(a, w, x)


def _prepare(p):
    depth = p["w_in"].shape[0]
    row = lambda g: g[:, None, :].astype(F32)
    w_in = p["w_in"]
    o = np.cumsum((0, NA_WIDTH, NA_WIDTH, NA_WIDTH, MLA_Q_RANK, MLA_KV_RANK, MLA_ROPE,
                   DIL_WIDTH, DIL_WIDTH, DIL_WIDTH, D_MODEL, D_MODEL, D_MODEL))
    seg = lambda a: w_in[:, :, o[a]:o[a + 1]]
    grp = lambda a, g: seg(a)[:, :, g * DIL_OUT:(g + 1) * DIL_OUT]
    rot = np.concatenate([np.arange(MLA_ROPE // 2, MLA_ROPE), np.arange(MLA_ROPE // 2)])
    zpad = jnp.zeros((depth, D_MODEL, LANES - MLA_ROPE), w_in.dtype)
    cols = [seg(0), seg(1), seg(2)]
    for g in range(len(DIL_GROUPS)):
        cols += [grp(6, g), grp(7, g), grp(8, g)]
    cols += [seg(3), seg(4), seg(5), zpad, seg(5)[:, :, rot], zpad, seg(9), seg(10), seg(11)]
    w_all = jnp.concatenate(cols, axis=-1).astype(BF16)

    tile = lambda g, reps: jnp.tile(g, (1, reps))
    ones = lambda n: jnp.ones((depth, n), F32)
    dil_gain = jnp.concatenate([tile(p["dil_q_norm"], DIL_HEADS_PER_GROUP), tile(p["dil_k_norm"], DIL_HEADS_PER_GROUP),
                                ones(DIL_OUT)], axis=-1)
    head_gain = jnp.concatenate(
        [tile(p["na_q_norm"], NA_HEADS), tile(p["na_k_norm"], NA_HEADS), ones(NA_WIDTH)]
        + [dil_gain] * len(DIL_GROUPS), axis=-1)

    wuq = p["mla_w_uq"].reshape(depth, MLA_Q_RANK, MLA_HEADS, MLA_QK)
    zq = jnp.zeros((depth, MLA_Q_RANK, MLA_HEADS, LANES - MLA_ROPE), wuq.dtype)
    wuq_rope = wuq[..., MLA_NOPE:]
    wuq = jnp.concatenate([wuq[..., :MLA_NOPE], wuq_rope, zq, wuq_rope[..., rot], zq], axis=-1)
    pad_gain = lambda g: jnp.concatenate([g, jnp.zeros((depth, MLA_QPAD - MLA_QK), F32)], axis=-1)

    return dict(
        ffn1=(row(p["ffn1_norm"]), p["ffn1_w_gate"].astype(BF16), p["ffn1_w_up"].astype(BF16),
              p["ffn1_w_down"].astype(BF16)),
        ffn2=(row(p["ffn2_norm"]), p["ffn2_w_gate"].astype(BF16), p["ffn2_w_up"].astype(BF16),
              p["ffn2_w_down"].astype(BF16)),
        mix_norm=row(p["mix_norm"]),
        w_all=w_all, head_gain=row(head_gain),
        na_bias=[_na_bias_tables(p["na_rpb"][l]) for l in range(depth)],
        cq_gain=row(p["mla_cq_norm"]), ckv_gain=row(p["mla_ckv_norm"]),
        wuq=wuq.reshape(depth, MLA_Q_RANK, MLA_HEADS * MLA_QEXT).astype(BF16),
        wukv=p["mla_w_ukv"].astype(BF16),
        q_gain=row(pad_gain(p["mla_q_norm"])), k_gain=row(pad_gain(p["mla_k_norm"])),
        w_na=p["w_na_out"].astype(BF16), w_mla=p["w_mla_out"].astype(BF16), w_dil=p["w_dil_out"].astype(BF16),
        w_o=p["w_o"].astype(BF16),
    )


def _dil_slopes():
    slopes = _alibi_slopes(DIL_HEADS)
    out = []
    for g, (_, dil) in enumerate(DIL_GROUPS):
        lo, hi = g * DIL_HEADS_PER_GROUP, (g + 1) * DIL_HEADS_PER_GROUP
        out.append(slopes[lo:hi] * dil)
    return out


def _layer(x, w, layer, b, s, rope, slopes):
    x = _ffn(x, *w["ffn1"], layer)
    qkv, aux = _inproj(x, w["mix_norm"], w["w_all"], w["head_gain"], layer)
    o_na = _na_attention(qkv, w["na_bias"][layer], b, s)
    q, k, v = _mla_prep(aux, rope[0], rope[1], w["cq_gain"], w["ckv_gain"], w["wuq"], w["wukv"],
                        w["q_gain"], w["k_gain"], layer, s)
    o_mla = _mla_attention(q, k, v, b, s)
    dil = [_dil_group_attention(qkv, slopes[g], g, b, s) for g in range(len(DIL_GROUPS))]
    merged = _merge(o_na, o_mla, [d[0] for d in dil], [d[1] for d in dil], aux,
                    w["w_na"], w["w_mla"], w["w_dil"], layer)
    x = _resid_matmul(merged, w["w_o"], x, layer)
    return _ffn(x, *w["ffn2"], layer)


def _trunk(x, w, slopes):
    b, s, d = x.shape
    rope = _rope_tables(s)
    y = x.reshape(b * s, d)
    for layer in range(w["w_all"].shape[0]):
        y = _layer(y, w, layer, b, s, rope, slopes)
    return y.reshape(b, s, d)


def kernel(x_prompt, x_sample, ffn1_norm, ffn1_w_gate, ffn1_w_up, ffn1_w_down, mix_norm, w_in, na_q_norm, na_k_norm, na_rpb, mla_cq_norm, mla_ckv_norm, mla_w_uq, mla_w_ukv, mla_q_norm, mla_k_norm, dil_q_norm, dil_k_norm, w_na_out, w_mla_out, w_dil_out, w_o, ffn2_norm, ffn2_w_gate, ffn2_w_up, ffn2_w_down):
    w = _prepare(dict(
        ffn1_norm=ffn1_norm, ffn1_w_gate=ffn1_w_gate, ffn1_w_up=ffn1_w_up, ffn1_w_down=ffn1_w_down,
        mix_norm=mix_norm, w_in=w_in, na_q_norm=na_q_norm, na_k_norm=na_k_norm, na_rpb=na_rpb,
        mla_cq_norm=mla_cq_norm, mla_ckv_norm=mla_ckv_norm, mla_w_uq=mla_w_uq, mla_w_ukv=mla_w_ukv,
        mla_q_norm=mla_q_norm, mla_k_norm=mla_k_norm, dil_q_norm=dil_q_norm, dil_k_norm=dil_k_norm,
        w_na_out=w_na_out, w_mla_out=w_mla_out, w_dil_out=w_dil_out, w_o=w_o,
        ffn2_norm=ffn2_norm, ffn2_w_gate=ffn2_w_gate, ffn2_w_up=ffn2_w_up, ffn2_w_down=ffn2_w_down))
    slopes = _dil_slopes()
    return (_trunk(x_prompt, w, slopes), _trunk(x_sample, w, slopes))
```

```python
import functools

import numpy as np
import jax
import jax.numpy as jnp
from jax import lax
from jax.experimental import pallas as pl
from jax.experimental.pallas import tpu as pltpu

F32 = jnp.float32
BF16 = jnp.bfloat16

D_MODEL = 2048
GRID_W = 64
HEAD_DIM = 128
EPS = 1e-6
NEG_INF = -1e30
LOG2E = 1.4426950408889634

NA_HEADS = 8
NA_WIN_R = 8
NA_WIN_C = 16
NA_WIDTH = NA_HEADS * HEAD_DIM

MLA_HEADS = 8
MLA_Q_RANK = 512
MLA_KV_RANK = 256
MLA_NOPE = 128
MLA_ROPE = 64
MLA_V = 128
MLA_QK = MLA_NOPE + MLA_ROPE
MLA_OUT = MLA_HEADS * MLA_V
ROPE_THETA = 10000.0

DIL_GROUPS = ((128, 1), (512, 4), (2048, 16))
DIL_HEADS_PER_GROUP = 4
DIL_HEADS = 12
DIL_WIDTH = DIL_HEADS * HEAD_DIM
DIL_OUT = DIL_HEADS_PER_GROUP * HEAD_DIM

D_FF = 5632

LANES = 128
VMEM_LIMIT = 56 * 1024 * 1024

IN_TM = 1024
IN_TN = 512
GATE_TILES = 3 * D_MODEL // IN_TN
QKV_TILE0 = GATE_TILES
QKV_TILES = (3 * NA_WIDTH + 3 * DIL_WIDTH) // IN_TN
NA_TILES = 3 * NA_WIDTH // IN_TN
BF_TILES = GATE_TILES + QKV_TILES
BF_COLS = BF_TILES * IN_TN
LAT_COLS = MLA_Q_RANK + MLA_KV_RANK + 2 * LANES
LAT_TILES = LAT_COLS // IN_TN
MLA_QPAD = 2 * LANES
MLA_QEXT = 3 * LANES


def _params(*sem):
    return pltpu.CompilerParams(dimension_semantics=sem, vmem_limit_bytes=VMEM_LIMIT)


def _rms(x, gain):
    return x * lax.rsqrt(jnp.mean(x * x, axis=-1, keepdims=True) + EPS) * gain


def _dot(a, b):
    return jnp.dot(a, b, preferred_element_type=F32)


def _dot_t(a, b):
    return lax.dot_general(a, b, (((1,), (1,)), ((), ())), preferred_element_type=F32)


def _ffn_kernel(x_ref, g_ref, wg_ref, wu_ref, wd_ref, o_ref, xn_ref):
    f = pl.program_id(1)

    @pl.when(f == 0)
    def _():
        x = x_ref[...]
        xn_ref[...] = _rms(x, g_ref[...]).astype(BF16)
        o_ref[...] = x

    xn = xn_ref[...]
    g = _dot(xn, wg_ref[...])
    u = _dot(xn, wu_ref[...])
    h = (g * jax.nn.sigmoid(g) * u * 0.5).astype(BF16)
    o_ref[...] += _dot(h, wd_ref[...])


def _ffn(x, gain, wg, wu, wd, layer, *, tm=1024, tf=512):
    t, d = x.shape
    ff = wg.shape[-1]
    return pl.pallas_call(
        _ffn_kernel,
        grid=(t // tm, ff // tf),
        in_specs=[
            pl.BlockSpec((tm, d), lambda i, f: (i, 0)),
            pl.BlockSpec((None, 1, d), lambda i, f: (layer, 0, 0)),
            pl.BlockSpec((None, d, tf), lambda i, f: (layer, 0, f)),
            pl.BlockSpec((None, d, tf), lambda i, f: (layer, 0, f)),
            pl.BlockSpec((None, tf, d), lambda i, f: (layer, f, 0)),
        ],
        out_specs=pl.BlockSpec((tm, d), lambda i, f: (i, 0)),
        out_shape=jax.ShapeDtypeStruct((t, d), F32),
        scratch_shapes=[pltpu.VMEM((tm, d), BF16)],
        compiler_params=_params("parallel", "arbitrary"),
        name="ffn",
    )(x, gain, wg, wu, wd)


def _inproj_kernel(x_ref, g_ref, w_ref, hg_ref, qkv_ref, aux_ref, yn_ref, xn_ref):
    j = pl.program_id(1)
    tm, d = x_ref.shape

    @pl.when(j == 0)
    def _():
        x = x_ref[...]
        inv = lax.rsqrt(jnp.mean(x * x, axis=-1, keepdims=True) + EPS)
        for c in range(d // LANES):
            cs = slice(c * LANES, (c + 1) * LANES)
            y = x_ref[:, cs] * inv * g_ref[:, cs]
            yn_ref[c] = y
            xn_ref[0, :, cs] = y.astype(BF16)
        for g in (1, 2):
            dil = DIL_GROUPS[g][1]
            rows = tm // dil
            for r in range(dil):
                for c in range(d // LANES):
                    xn_ref[g, r * rows:(r + 1) * rows, c * LANES:(c + 1) * LANES] = (
                        yn_ref[c, pl.ds(r, rows, stride=dil), :].astype(BF16))

    jq = j - QKV_TILE0
    is_gate = j < QKV_TILE0
    is_qkv = (jq >= 0) & (jq < QKV_TILES)
    normed = is_qkv & ((jq < 4) | ((jq >= NA_TILES) & ((jq - NA_TILES) % 3 != 2)))
    order = jnp.where(jq < NA_TILES + 3, 0, jnp.where(jq < NA_TILES + 6, 1, 2))

    @pl.when(is_gate)
    def _():
        qkv_ref[...] = jax.nn.sigmoid(_dot(xn_ref[0], w_ref[...])).astype(BF16)

    @pl.when(normed)
    def _():
        acc = _dot(xn_ref[order], w_ref[...])
        hg = hg_ref[...]
        for c in range(acc.shape[1] // HEAD_DIM):
            sl = slice(c * HEAD_DIM, (c + 1) * HEAD_DIM)
            qkv_ref[:, sl] = _rms(acc[:, sl], hg[:, sl]).astype(BF16)

    @pl.when(is_qkv & jnp.logical_not(normed))
    def _():
        qkv_ref[...] = _dot(xn_ref[order], w_ref[...]).astype(BF16)

    @pl.when(j >= BF_TILES)
    def _():
        aux_ref[...] = _dot(xn_ref[0], w_ref[...])


def _inproj(x, gain, w, head_gain, layer):
    t, d = x.shape
    tm, tn = IN_TM, IN_TN
    return pl.pallas_call(
        _inproj_kernel,
        grid=(t // tm, BF_TILES + LAT_TILES),
        in_specs=[
            pl.BlockSpec((tm, d), lambda i, j: (i, 0)),
            pl.BlockSpec((None, 1, d), lambda i, j: (layer, 0, 0)),
            pl.BlockSpec((None, d, tn), lambda i, j: (layer, 0, j)),
            pl.BlockSpec((None, 1, tn), lambda i, j: (layer, 0, jnp.clip(j - QKV_TILE0, 0, QKV_TILES - 1))),
        ],
        out_specs=[
            pl.BlockSpec((tm, tn), lambda i, j: (i, jnp.minimum(j, BF_TILES - 1))),
            pl.BlockSpec((tm, tn), lambda i, j: (i, jnp.maximum(j - BF_TILES, 0))),
        ],
        out_shape=[jax.ShapeDtypeStruct((t, BF_COLS), BF16), jax.ShapeDtypeStruct((t, LAT_COLS), F32)],
        scratch_shapes=[pltpu.VMEM((d // LANES, tm, LANES), F32), pltpu.VMEM((3, tm, d), BF16)],
        compiler_params=_params("parallel", "arbitrary"),
        name="inproj",
    )(x, gain, w, head_gain)


NA_QROWS = 4
NA_TQ = NA_QROWS * GRID_W
NA_KROWS = 3 * NA_QROWS
NA_TK = NA_KROWS * GRID_W


def _na_bias_tables(rpb):
    heads = rpb.shape[0]
    qc = np.arange(GRID_W)[:, None]
    kc = np.arange(GRID_W)[None, :]
    col_start = np.clip(qc - NA_WIN_C // 2, 0, GRID_W - NA_WIN_C)
    col_ok = (kc >= col_start) & (kc < col_start + NA_WIN_C)
    dc_idx = np.clip(kc - qc, -(NA_WIN_C - 1), NA_WIN_C - 1) + NA_WIN_C - 1
    onehot = (dc_idx[None] == np.arange(2 * NA_WIN_C - 1)[:, None, None]).astype(np.float32)
    by_col = jnp.einsum("hrd,dqk->hqrk", rpb, jnp.asarray(onehot), precision=lax.Precision.HIGHEST)
    by_col = jnp.where(jnp.asarray(col_ok)[None, :, None, :], by_col, NEG_INF)
    tables = []
    for q_rel, k0_rel in ((0, None), (NA_QROWS, 0), (2 * NA_QROWS, None)):
        rows = []
        for j in range(NA_QROWS):
            k0 = j if k0_rel == 0 else (0 if q_rel == 0 else NA_QROWS)
            dr0 = k0 - (q_rel + j) + NA_WIN_R - 1
            win = by_col[:, :, dr0:dr0 + NA_WIN_R, :]
            pad = lambda n: jnp.full((heads, GRID_W, n, GRID_W), NEG_INF, F32)
            rows.append(jnp.concatenate([pad(k0), win, pad(NA_KROWS - NA_WIN_R - k0)], axis=2))
        tables.append(jnp.stack(rows, axis=1).reshape(heads, NA_TQ, NA_TK))
    return jnp.stack(tables)


def _na_kernel(q_ref, k0_ref, k1_ref, k2_ref, v0_ref, v1_ref, v2_ref, bias_ref, o_ref):
    scale = HEAD_DIM ** -0.5
    for h in range(NA_HEADS):
        sl = slice(h * HEAD_DIM, (h + 1) * HEAD_DIM)
        q = q_ref[:, sl]
        s = jnp.concatenate([_dot_t(q, k_ref[:, sl]) for k_ref in (k0_ref, k1_ref, k2_ref)], axis=1)
        s = s * scale + bias_ref[h]
        m = jnp.max(s, axis=-1, keepdims=True)
        e = jnp.exp(s - m)
        p = (e * (1.0 / jnp.sum(e, axis=-1, keepdims=True))).astype(BF16)
        o = _dot(p[:, 0:NA_TQ], v0_ref[:, sl])
        o += _dot(p[:, NA_TQ:2 * NA_TQ], v1_ref[:, sl])
        o += _dot(p[:, 2 * NA_TQ:3 * NA_TQ], v2_ref[:, sl])
        o_ref[:, sl] = o.astype(o_ref.dtype)


def _na_attention(qkv, bias, b, s):
    nb = s // NA_TQ
    assert nb >= 3
    q_col = QKV_TILE0 * IN_TN // NA_WIDTH
    k_col, v_col = q_col + 1, q_col + 2

    def win(i):
        return jnp.clip(i - 1, 0, nb - 3)

    def kv_spec(col, t):
        return pl.BlockSpec((NA_TQ, NA_WIDTH), lambda bi, i: (bi * nb + win(i) + t, col))

    def bias_map(bi, i):
        kind = jnp.where(i == 0, 0, jnp.where(i == nb - 1, 2, 1))
        return (kind, 0, 0, 0)

    return pl.pallas_call(
        _na_kernel,
        grid=(b, nb),
        in_specs=[pl.BlockSpec((NA_TQ, NA_WIDTH), lambda bi, i: (bi * nb + i, q_col))]
        + [kv_spec(k_col, t) for t in range(3)]
        + [kv_spec(v_col, t) for t in range(3)]
        + [pl.BlockSpec((None, NA_HEADS, NA_TQ, NA_TK), bias_map)],
        out_specs=pl.BlockSpec((NA_TQ, NA_WIDTH), lambda bi, i: (bi * nb + i, 0)),
        out_shape=jax.ShapeDtypeStruct((b * s, NA_WIDTH), BF16),
        compiler_params=_params("parallel", "arbitrary"),
        name="na_attn",
    )(qkv, qkv, qkv, qkv, qkv, qkv, qkv, bias)


def _rope_tables(s):
    half = MLA_ROPE // 2
    inv = ROPE_THETA ** (-jnp.arange(half, dtype=F32) / half)
    ang = jnp.arange(s).astype(F32)[:, None] * inv[None, :]
    cos, sin = jnp.cos(ang), jnp.sin(ang)
    zero = jnp.zeros((s, LANES - MLA_ROPE), F32)
    return (jnp.concatenate([cos, cos, zero], axis=-1), jnp.concatenate([-sin, sin, zero], axis=-1))


def _mla_prep_kernel(cq_ref, ckv_ref, kr_ref, krr_ref, cos_ref, sin_ref, cqg_ref, ckvg_ref, wuq_ref, wukv_ref,
                     qg_ref, kg_ref, q_out, k_out, v_out):
    cos = cos_ref[...]
    sin = sin_ref[...]
    qf = _dot(_rms(cq_ref[...], cqg_ref[...]).astype(BF16), wuq_ref[...])
    kvf = _dot(_rms(ckv_ref[...], ckvg_ref[...]).astype(BF16), wukv_ref[...])
    kr = kr_ref[...] * cos + krr_ref[...] * sin
    kr_ss = jnp.sum(kr * kr, axis=-1, keepdims=True)
    qg = qg_ref[...]
    kg = kg_ref[...]
    for h in range(MLA_HEADS):
        qb = h * MLA_QEXT
        qn = qf[:, qb:qb + LANES]
        qr = qf[:, qb + LANES:qb + 2 * LANES] * cos + qf[:, qb + 2 * LANES:qb + 3 * LANES] * sin
        ms = (jnp.sum(qn * qn, axis=-1, keepdims=True) + jnp.sum(qr * qr, axis=-1, keepdims=True)) * (1.0 / MLA_QK)
        r = lax.rsqrt(ms + EPS)
        ob = h * MLA_QPAD
        q_out[:, ob:ob + LANES] = (qn * r * qg[:, :LANES]).astype(BF16)
        q_out[:, ob + LANES:ob + 2 * LANES] = (qr * r * qg[:, LANES:]).astype(BF16)
        kb = h * (MLA_NOPE + MLA_V)
        kn = kvf[:, kb:kb + MLA_NOPE]
        msk = (jnp.sum(kn * kn, axis=-1, keepdims=True) + kr_ss) * (1.0 / MLA_QK)
        rk = lax.rsqrt(msk + EPS)
        k_out[:, ob:ob + LANES] = (kn * rk * kg[:, :LANES]).astype(BF16)
        k_out[:, ob + LANES:ob + 2 * LANES] = (kr * rk * kg[:, LANES:]).astype(BF16)
        v_out[:, h * MLA_V:(h + 1) * MLA_V] = kvf[:, kb + MLA_NOPE:kb + MLA_NOPE + MLA_V].astype(BF16)


def _mla_prep(aux, cos, sin, cq_gain, ckv_gain, wuq, wukv, q_gain, k_gain, layer, s, *, tm=256):
    t = aux.shape[0]
    n_pos = s // tm
    full = lambda shape: pl.BlockSpec((None,) + shape, lambda i: (layer,) + (0,) * len(shape))
    return pl.pallas_call(
        _mla_prep_kernel,
        grid=(t // tm,),
        in_specs=[
            pl.BlockSpec((tm, MLA_Q_RANK), lambda i: (i, 0)),
            pl.BlockSpec((tm, MLA_KV_RANK), lambda i: (i, MLA_Q_RANK // MLA_KV_RANK)),
            pl.BlockSpec((tm, LANES), lambda i: (i, (MLA_Q_RANK + MLA_KV_RANK) // LANES)),
            pl.BlockSpec((tm, LANES), lambda i: (i, (MLA_Q_RANK + MLA_KV_RANK) // LANES + 1)),
            pl.BlockSpec((tm, LANES), lambda i: (i % n_pos, 0)),
            pl.BlockSpec((tm, LANES), lambda i: (i % n_pos, 0)),
            full((1, MLA_Q_RANK)),
            full((1, MLA_KV_RANK)),
            full((MLA_Q_RANK, MLA_HEADS * MLA_QEXT)),
            full((MLA_KV_RANK, MLA_HEADS * (MLA_NOPE + MLA_V))),
            full((1, MLA_QPAD)),
            full((1, MLA_QPAD)),
        ],
        out_specs=[
            pl.BlockSpec((tm, MLA_HEADS * MLA_QPAD), lambda i: (i, 0)),
            pl.BlockSpec((tm, MLA_HEADS * MLA_QPAD), lambda i: (i, 0)),
            pl.BlockSpec((tm, MLA_OUT), lambda i: (i, 0)),
        ],
        out_shape=[
            jax.ShapeDtypeStruct((t, MLA_HEADS * MLA_QPAD), BF16),
            jax.ShapeDtypeStruct((t, MLA_HEADS * MLA_QPAD), BF16),
            jax.ShapeDtypeStruct((t, MLA_OUT), BF16),
        ],
        compiler_params=_params("parallel"),
        name="mla_prep",
    )(aux, aux, aux, aux, cos, sin, cq_gain, ckv_gain, wuq, wukv, q_gain, k_gain)


MLA_TQ = 1024
MLA_KCHUNK = 512


def _lane_groups(x, op):
    out = x[:, :LANES]
    for c in range(1, x.shape[1] // LANES):
        out = op(out, x[:, c * LANES:(c + 1) * LANES])
    return out


def _mla_flash_kernel(q_ref, k_ref, v_ref, o_ref, m_ref, l_ref, acc_ref):
    ki = pl.program_id(3)

    @pl.when(ki == 0)
    def _():
        m_ref[...] = jnp.full_like(m_ref, -jnp.inf)
        l_ref[...] = jnp.zeros_like(l_ref)
        acc_ref[...] = jnp.zeros_like(acc_ref)

    q = q_ref[...]
    c2 = (MLA_QK ** -0.5) * LOG2E
    m, l, acc = m_ref[...], l_ref[...], acc_ref[...]
    for c in range(k_ref.shape[0] // MLA_KCHUNK):
        ks = slice(c * MLA_KCHUNK, (c + 1) * MLA_KCHUNK)
        s = _dot_t(q, k_ref[ks, :]) * c2
        m_new = jnp.maximum(m, jnp.max(_lane_groups(s, jnp.maximum), axis=-1, keepdims=True))
        alpha = jnp.exp2(m - m_new)
        p = jnp.exp2(s - jnp.concatenate([m_new] * (MLA_KCHUNK // LANES), axis=1))
        l = alpha * l + _lane_groups(p, jnp.add)
        acc = alpha * acc + _dot(p.astype(BF16), v_ref[ks, :])
        m = m_new
    m_ref[...] = m
    l_ref[...] = l
    acc_ref[...] = acc

    @pl.when(ki == pl.num_programs(3) - 1)
    def _():
        o_ref[...] = (acc * (1.0 / jnp.sum(l, axis=-1, keepdims=True))).astype(o_ref.dtype)


def _mla_attention(q, k, v, b, s, *, tk=8192):
    tq, tk = MLA_TQ, min(tk, s)
    nq, nk = s // tq, s // tk
    stat = pltpu.VMEM((tq, LANES), F32)
    return pl.pallas_call(
        _mla_flash_kernel,
        grid=(b, MLA_HEADS, nq, nk),
        in_specs=[
            pl.BlockSpec((tq, MLA_QPAD), lambda bi, h, qi, ki: (bi * nq + qi, h)),
            pl.BlockSpec((tk, MLA_QPAD), lambda bi, h, qi, ki: (bi * nk + ki, h)),
            pl.BlockSpec((tk, MLA_V), lambda bi, h, qi, ki: (bi * nk + ki, h)),
        ],
        out_specs=pl.BlockSpec((tq, MLA_V), lambda bi, h, qi, ki: (bi * nq + qi, h)),
        out_shape=jax.ShapeDtypeStruct((b * s, MLA_OUT), BF16),
        scratch_shapes=[stat, stat, pltpu.VMEM((tq, MLA_V), F32)],
        compiler_params=_params("parallel", "parallel", "parallel", "arbitrary"),
        name="mla_flash",
    )(q, k, v)


DIL_TQ = 256
DIL_KBLOCKS = 4


def _alibi_slopes(n):
    return 2.0 ** (-8.0 * jnp.arange(1, n + 1, dtype=F32) / n)


def _dil_kernel(slope_ref, q_ref, *refs, n, radius, tq):
    k_refs, v_refs = refs[:DIL_KBLOCKS], refs[DIL_KBLOCKS:2 * DIL_KBLOCKS]
    o_ref, lse_ref = refs[2 * DIL_KBLOCKS:]
    kb = tq // 2
    tk = DIL_KBLOCKS * kb
    i = pl.program_id(2)
    row = lax.broadcasted_iota(jnp.int32, (tq, tk), 0)
    col = lax.broadcasted_iota(jnp.int32, (tq, tk), 1)
    kpos = (2 * i - 1) * kb + col
    dist_i = jnp.abs(row + kb - col)
    valid = (kpos >= 0) & (kpos < n) & (dist_i <= radius)
    dist = dist_i.astype(F32)
    scale = HEAD_DIM ** -0.5
    for h in range(DIL_HEADS_PER_GROUP):
        sl = slice(h * HEAD_DIM, (h + 1) * HEAD_DIM)
        q = q_ref[..., sl].reshape(tq, HEAD_DIM)
        k = jnp.concatenate([ref[..., sl].reshape(kb, HEAD_DIM) for ref in k_refs], axis=0)
        v = jnp.concatenate([ref[..., sl].reshape(kb, HEAD_DIM) for ref in v_refs], axis=0)
        s = _dot_t(q, k) * scale - slope_ref[h] * dist
        s = jnp.where(valid, s, NEG_INF)
        m = jnp.max(s, axis=-1, keepdims=True)
        p = jnp.exp(s - m)
        den = jnp.sum(p, axis=-1, keepdims=True)
        o = _dot(p.astype(BF16), v) * (1.0 / den)
        o_ref[..., sl] = o.reshape(o_ref.shape[:-1] + (HEAD_DIM,))
        lse_ref[..., sl] = jnp.broadcast_to(m + jnp.log(den), (tq, HEAD_DIM)).reshape(o_ref.shape[:-1] + (HEAD_DIM,))


def _dil_group_attention(qkv, slopes, group, b, s):
    window, dil = DIL_GROUPS[group]
    radius = (window // 2) // dil
    n = s // dil
    tq = min(DIL_TQ, n)
    kb = tq // 2
    nblk = n // tq
    assert radius <= kb and s % IN_TM == 0 and n % tq == 0
    t = b * s
    rows = IN_TM // dil
    tiles = s // IN_TM
    q_unit = QKV_TILE0 + NA_TILES + 3 * group

    def spec(br, unit, pick):
        if rows >= br:
            sub = rows // br
            block = (None, None, br, DIL_OUT)
            index = lambda bi, r, u: (bi * tiles + u // sub, r, u % sub, unit)
        else:
            span = br // rows
            assert tiles % span == 0
            block = (span, None, rows, DIL_OUT)
            index = lambda bi, r, u: (bi * tiles // span + u, r, 0, unit)
        return pl.BlockSpec(block, lambda bi, r, i: index(bi, r, pick(i)))

    def kv_specs(unit):
        return [spec(kb, unit, lambda i, j=j: jnp.clip(2 * i - 1 + j, 0, n // kb - 1)) for j in range(DIL_KBLOCKS)]

    qkv_v = qkv.reshape(t // IN_TM, dil, rows, BF_COLS)
    kern = functools.partial(_dil_kernel, n=n, radius=radius, tq=tq)
    out_spec = spec(tq, 0, lambda i: i)
    o, lse = pl.pallas_call(
        kern,
        grid=(b, dil, nblk),
        in_specs=[pl.BlockSpec(memory_space=pltpu.SMEM), spec(tq, q_unit, lambda i: i)]
        + kv_specs(q_unit + 1) + kv_specs(q_unit + 2),
        out_specs=[out_spec, out_spec],
        out_shape=[jax.ShapeDtypeStruct((t // IN_TM, dil, rows, DIL_OUT), F32)] * 2,
        compiler_params=_params("parallel", "parallel", "arbitrary"),
        name=f"dil_attn_g{group}",
    )(slopes, *([qkv_v] * (1 + 2 * DIL_KBLOCKS)))
    return o, lse


MERGE_TM = 512
NCH = DIL_OUT // LANES


def _merge_kernel(ona_ref, omla_ref, od0_ref, od1_ref, od2_ref, ls0_ref, ls1_ref, ls2_ref,
                  gna_ref, gmla_ref, gdil_ref, wna_ref, wmla_ref, wdil_ref, o_ref, od_ref, tok_ref):
    j = pl.program_id(1)
    tm = MERGE_TM

    @pl.when(j == 0)
    def _():
        for a, (ref, dil) in enumerate(((od1_ref, 4), (ls1_ref, 4), (od2_ref, 16), (ls2_ref, 16))):
            for r in range(dil):
                for c in range(NCH):
                    tok_ref[a, c, pl.ds(r, tm // dil, stride=dil), :] = ref[r, :, c * LANES:(c + 1) * LANES]
        for c in range(NCH):
            cs = slice(c * LANES, (c + 1) * LANES)
            od0, ls0 = od0_ref[0, :, cs], ls0_ref[0, :, cs]
            od1, ls1, od2, ls2 = tok_ref[0, c], tok_ref[1, c], tok_ref[2, c], tok_ref[3, c]
            top = jnp.maximum(jnp.maximum(ls0, ls1), ls2)
            w0, w1, w2 = jnp.exp(ls0 - top), jnp.exp(ls1 - top), jnp.exp(ls2 - top)
            inv = 1.0 / (w0 + w1 + w2)
            od = (w0 * inv) * od0 + (w1 * inv) * od1 + (w2 * inv) * od2
            od_ref[:, cs] = od.astype(BF16)

    gate = lambda ref: ref[...].astype(F32)
    merged = gate(gna_ref) * _dot(ona_ref[...], wna_ref[j])
    merged += gate(gmla_ref) * _dot(omla_ref[...], wmla_ref[j])
    merged += gate(gdil_ref) * _dot(od_ref[...], wdil_ref[j])
    o_ref[...] = merged.astype(o_ref.dtype)


def _col_tiles(w, tn):
    depth, k, n = w.shape
    return w.reshape(depth, k, n // tn, tn).transpose(0, 2, 1, 3)


def _resident(w, layer):
    return pl.BlockSpec((None,) + w.shape[1:], lambda i, j: (layer, 0, 0, 0))


def _merge(o_na, o_mla, o_dil, lse_dil, gates, w_na, w_mla, w_dil, layer):
    t = o_na.shape[0]
    tm = MERGE_TM
    tn = w_na.shape[-1]
    halves = IN_TM // tm
    gstep = D_MODEL // tn
    row = lambda width: pl.BlockSpec((tm, width), lambda i, j: (i, 0))
    gate = lambda g: pl.BlockSpec((tm, tn), lambda i, j: (i, g * gstep + j))

    def dil_spec(g):
        dil = DIL_GROUPS[g][1]
        return pl.BlockSpec((None, dil, tm // dil, DIL_OUT), lambda i, j: (i // halves, 0, i % halves, 0))

    return pl.pallas_call(
        _merge_kernel,
        grid=(t // tm, D_MODEL // tn),
        in_specs=[row(NA_WIDTH), row(MLA_OUT)] + [dil_spec(g) for g in range(3)] * 2
        + [gate(0), gate(1), gate(2)] + [_resident(w, layer) for w in (w_na, w_mla, w_dil)],
        out_specs=pl.BlockSpec((tm, tn), lambda i, j: (i, j)),
        out_shape=jax.ShapeDtypeStruct((t, D_MODEL), BF16),
        scratch_shapes=[pltpu.VMEM((tm, DIL_OUT), BF16), pltpu.VMEM((4, NCH, tm, LANES), F32)],
        compiler_params=_params("parallel", "arbitrary"),
        name="merge",
    )(o_na, o_mla, *o_dil, *lse_dil, gates, gates, gates, w_na, w_mla, w_dil)


def _resid_matmul_kernel(a_ref, w_ref, x_ref, o_ref):
    o_ref[...] = x_ref[...] + _dot(a_ref[...], w_ref[pl.program_id(1)])


def _resid_matmul(a, w, x, layer, *, tm=1024):
    t, k = a.shape
    tn = w.shape[-1]
    n = w.shape[1] * tn
    return pl.pallas_call(
        _resid_matmul_kernel,
        grid=(t // tm, n // tn),
        in_specs=[
            pl.BlockSpec((tm, k), lambda i, j: (i, 0)),
            _resident(w, layer),
            pl.BlockSpec((tm, tn), lambda i, j: (i, j)),
        ],
        out_specs=pl.BlockSpec((tm, tn), lambda i, j: (i, j)),
        out_shape=jax.ShapeDtypeStruct((t, n), F32),
        compiler_params=_params("parallel", "arbitrary"),
        name="wo_resid",
    )(a, w, x)


def _prepare(p):
    depth = p["w_in"].shape[0]
    row = lambda g: g[:, None, :].astype(F32)
    w_in = p["w_in"]
    o = np.cumsum((0, NA_WIDTH, NA_WIDTH, NA_WIDTH, MLA_Q_RANK, MLA_KV_RANK, MLA_ROPE,
                   DIL_WIDTH, DIL_WIDTH, DIL_WIDTH, D_MODEL, D_MODEL, D_MODEL))
    seg = lambda a: w_in[:, :, o[a]:o[a + 1]]
    grp = lambda a, g: seg(a)[:, :, g * DIL_OUT:(g + 1) * DIL_OUT]
    rot = np.concatenate([np.arange(MLA_ROPE // 2, MLA_ROPE), np.arange(MLA_ROPE // 2)])
    zpad = jnp.zeros((depth, D_MODEL, LANES - MLA_ROPE), w_in.dtype)
    cols = [seg(9), seg(10), seg(11), seg(0), seg(1), seg(2)]
    for g in range(len(DIL_GROUPS)):
        cols += [grp(6, g), grp(7, g), grp(8, g)]
    cols += [seg(3), seg(4), seg(5), zpad, seg(5)[:, :, rot], zpad]
    w_all = jnp.concatenate(cols, axis=-1).astype(BF16)

    tile = lambda g, reps: jnp.tile(g, (1, reps))
    ones = lambda n: jnp.ones((depth, n), F32)
    dil_gain = jnp.concatenate([tile(p["dil_q_norm"], DIL_HEADS_PER_GROUP), tile(p["dil_k_norm"], DIL_HEADS_PER_GROUP),
                                ones(DIL_OUT)], axis=-1)
    head_gain = jnp.concatenate(
        [tile(p["na_q_norm"], NA_HEADS), tile(p["na_k_norm"], NA_HEADS), ones(NA_WIDTH)]
        + [dil_gain] * len(DIL_GROUPS), axis=-1)

    wuq = p["mla_w_uq"].reshape(depth, MLA_Q_RANK, MLA_HEADS, MLA_QK)
    zq = jnp.zeros((depth, MLA_Q_RANK, MLA_HEADS, LANES - MLA_ROPE), wuq.dtype)
    wuq_rope = wuq[..., MLA_NOPE:]
    wuq = jnp.concatenate([wuq[..., :MLA_NOPE], wuq_rope, zq, wuq_rope[..., rot], zq], axis=-1)
    pad_gain = lambda g: jnp.concatenate([g, jnp.zeros((depth, MLA_QPAD - MLA_QK), F32)], axis=-1)

    return dict(
        ffn1=(row(p["ffn1_norm"]), p["ffn1_w_gate"].astype(BF16), p["ffn1_w_up"].astype(BF16),
              p["ffn1_w_down"].astype(BF16)),
        ffn2=(row(p["ffn2_norm"]), p["ffn2_w_gate"].astype(BF16), p["ffn2_w_up"].astype(BF16),
              p["ffn2_w_down"].astype(BF16)),
        mix_norm=row(p["mix_norm"]),
        w_all=w_all, head_gain=row(head_gain),
        na_bias=[_na_bias_tables(p["na_rpb"][l]) for l in range(depth)],
        cq_gain=row(p["mla_cq_norm"]), ckv_gain=row(p["mla_ckv_norm"]),
        wuq=wuq.reshape(depth, MLA_Q_RANK, MLA_HEADS * MLA_QEXT).astype(BF16),
        wukv=p["mla_w_ukv"].astype(BF16),
        q_gain=row(pad_gain(p["mla_q_norm"])), k_gain=row(pad_gain(p["mla_k_norm"])),
        w_na=_col_tiles(p["w_na_out"].astype(BF16), IN_TN), w_mla=_col_tiles(p["w_mla_out"].astype(BF16), IN_TN),
        w_dil=_col_tiles(p["w_dil_out"].astype(BF16), IN_TN), w_o=_col_tiles(p["w_o"].astype(BF16), IN_TN),
    )


def _dil_slopes():
    slopes = _alibi_slopes(DIL_HEADS)
    out = []
    for g, (_, dil) in enumerate(DIL_GROUPS):
        lo, hi = g * DIL_HEADS_PER_GROUP, (g + 1) * DIL_HEADS_PER_GROUP
        out.append(slopes[lo:hi] * dil)
    return out


def _layer(x, w, layer, b, s, rope, slopes):
    x = _ffn(x, *w["ffn1"], layer)
    qkv, aux = _inproj(x, w["mix_norm"], w["w_all"], w["head_gain"], layer)
    o_na = _na_attention(qkv, w["na_bias"][layer], b, s)
    q, k, v = _mla_prep(aux, rope[0], rope[1], w["cq_gain"], w["ckv_gain"], w["wuq"], w["wukv"],
                        w["q_gain"], w["k_gain"], layer, s)
    o_mla = _mla_attention(q, k, v, b, s)
    dil = [_dil_group_attention(qkv, slopes[g], g, b, s) for g in range(len(DIL_GROUPS))]
    merged = _merge(o_na, o_mla, [d[0] for d in dil], [d[1] for d in dil], qkv,
                    w["w_na"], w["w_mla"], w["w_dil"], layer)
    x = _resid_matmul(merged, w["w_o"], x, layer)
    return _ffn(x, *w["ffn2"], layer)


def _trunk(x, w, slopes):
    b, s, d = x.shape
    rope = _rope_tables(s)
    y = x.reshape(b * s, d)
    for layer in range(w["w_all"].shape[0]):
        y = _layer(y, w, layer, b, s, rope, slopes)
    return y.reshape(b, s, d)


def kernel(x_prompt, x_sample, ffn1_norm, ffn1_w_gate, ffn1_w_up, ffn1_w_down, mix_norm, w_in, na_q_norm, na_k_norm, na_rpb, mla_cq_norm, mla_ckv_norm, mla_w_uq, mla_w_ukv, mla_q_norm, mla_k_norm, dil_q_norm, dil_k_norm, w_na_out, w_mla_out, w_dil_out, w_o, ffn2_norm, ffn2_w_gate, ffn2_w_up, ffn2_w_down):
    w = _prepare(dict(
        ffn1_norm=ffn1_norm, ffn1_w_gate=ffn1_w_gate, ffn1_w_up=ffn1_w_up, ffn1_w_down=ffn1_w_down,
        mix_norm=mix_norm, w_in=w_in, na_q_norm=na_q_norm, na_k_norm=na_k_norm, na_rpb=na_rpb,
        mla_cq_norm=mla_cq_norm, mla_ckv_norm=mla_ckv_norm, mla_w_uq=mla_w_uq, mla_w_ukv=mla_w_ukv,
        mla_q_norm=mla_q_norm, mla_k_norm=mla_k_norm, dil_q_norm=dil_q_norm, dil_k_norm=dil_k_norm,
        w_na_out=w_na_out, w_mla_out=w_mla_out, w_dil_out=w_dil_out, w_o=w_o,
        ffn2_norm=ffn2_norm, ffn2_w_gate=ffn2_w_gate, ffn2_w_up=ffn2_w_up, ffn2_w_down=ffn2_w_down))
    slopes = _dil_slopes()
    return (_trunk(x_prompt, w, slopes), _trunk(x_sample, w, slopes))
```

```python
import functools

import numpy as np
import jax
import jax.numpy as jnp
from jax import lax
from jax.experimental import pallas as pl
from jax.experimental.pallas import tpu as pltpu

F32 = jnp.float32
BF16 = jnp.bfloat16

D_MODEL = 2048
GRID_W = 64
HEAD_DIM = 128
EPS = 1e-6
NEG_INF = -1e30
LOG2E = 1.4426950408889634

NA_HEADS = 8
NA_WIN_R = 8
NA_WIN_C = 16
NA_WIDTH = NA_HEADS * HEAD_DIM

MLA_HEADS = 8
MLA_Q_RANK = 512
MLA_KV_RANK = 256
MLA_NOPE = 128
MLA_ROPE = 64
MLA_V = 128
MLA_QK = MLA_NOPE + MLA_ROPE
MLA_OUT = MLA_HEADS * MLA_V
ROPE_THETA = 10000.0

DIL_GROUPS = ((128, 1), (512, 4), (2048, 16))
DIL_HEADS_PER_GROUP = 4
DIL_HEADS = 12
DIL_WIDTH = DIL_HEADS * HEAD_DIM
DIL_OUT = DIL_HEADS_PER_GROUP * HEAD_DIM

D_FF = 5632
FFN_TF = 512

LANES = 128
VMEM_LIMIT = 56 * 1024 * 1024

IN_TM = 1024
IN_TN = 512
GATE_TILES = 3 * D_MODEL // IN_TN
QKV_TILE0 = GATE_TILES
QKV_TILES = (3 * NA_WIDTH + 3 * DIL_WIDTH) // IN_TN
NA_TILES = 3 * NA_WIDTH // IN_TN
BF_TILES = GATE_TILES + QKV_TILES
BF_COLS = BF_TILES * IN_TN
LAT_COLS = MLA_Q_RANK + MLA_KV_RANK + 2 * LANES
LAT_TILES = LAT_COLS // IN_TN
MLA_QPAD = 2 * LANES
MLA_QEXT = 3 * LANES


def _params(*sem):
    return pltpu.CompilerParams(dimension_semantics=sem, vmem_limit_bytes=VMEM_LIMIT)


def _rms(x, gain):
    return x * lax.rsqrt(jnp.mean(x * x, axis=-1, keepdims=True) + EPS) * gain


def _dot(a, b):
    return jnp.dot(a, b, preferred_element_type=F32)


def _dot_t(a, b):
    return lax.dot_general(a, b, (((1,), (1,)), ((), ())), preferred_element_type=F32)


def _ffn_kernel(x_ref, g_ref, wg_ref, wu_ref, wd_ref, o_ref, xn_ref):
    f = pl.program_id(1)

    @pl.when(f == 0)
    def _():
        x = x_ref[...]
        xn_ref[...] = _rms(x, g_ref[...]).astype(BF16)
        o_ref[...] = x

    xn = xn_ref[...]
    g = _dot(xn, wg_ref[...])
    u = _dot(xn, wu_ref[...])
    h = (g * jax.nn.sigmoid(g) * u * 0.5).astype(BF16)
    o_ref[...] += _dot(h, wd_ref[...])


def _ffn(x, gain, wg, wu, wd, layer, *, tm=1024):
    t, d = x.shape
    tf = wg.shape[-1]
    ff = wg.shape[1] * tf
    return pl.pallas_call(
        _ffn_kernel,
        grid=(t // tm, ff // tf),
        in_specs=[
            pl.BlockSpec((tm, d), lambda i, f: (i, 0)),
            pl.BlockSpec((None, 1, d), lambda i, f: (layer, 0, 0)),
            pl.BlockSpec((None, None, d, tf), lambda i, f: (layer, f, 0, 0)),
            pl.BlockSpec((None, None, d, tf), lambda i, f: (layer, f, 0, 0)),
            pl.BlockSpec((None, tf, d), lambda i, f: (layer, f, 0)),
        ],
        out_specs=pl.BlockSpec((tm, d), lambda i, f: (i, 0)),
        out_shape=jax.ShapeDtypeStruct((t, d), F32),
        scratch_shapes=[pltpu.VMEM((tm, d), BF16)],
        compiler_params=_params("parallel", "arbitrary"),
        name="ffn",
    )(x, gain, wg, wu, wd)


def _inproj_kernel(x_ref, g_ref, w_ref, hg_ref, qkv_ref, aux_ref, yn_ref, xn_ref):
    j = pl.program_id(1)
    tm, d = x_ref.shape

    @pl.when(j == 0)
    def _():
        x = x_ref[...]
        inv = lax.rsqrt(jnp.mean(x * x, axis=-1, keepdims=True) + EPS)
        for c in range(d // LANES):
            cs = slice(c * LANES, (c + 1) * LANES)
            y = x_ref[:, cs] * inv * g_ref[:, cs]
            yn_ref[c] = y
            xn_ref[0, :, cs] = y.astype(BF16)
        for g in (1, 2):
            dil = DIL_GROUPS[g][1]
            rows = tm // dil
            for r in range(dil):
                for c in range(d // LANES):
                    xn_ref[g, r * rows:(r + 1) * rows, c * LANES:(c + 1) * LANES] = (
                        yn_ref[c, pl.ds(r, rows, stride=dil), :].astype(BF16))

    jq = j - QKV_TILE0
    is_gate = j < QKV_TILE0
    is_qkv = (jq >= 0) & (jq < QKV_TILES)
    normed = is_qkv & ((jq < 4) | ((jq >= NA_TILES) & ((jq - NA_TILES) % 3 != 2)))
    order = jnp.where(jq < NA_TILES + 3, 0, jnp.where(jq < NA_TILES + 6, 1, 2))

    @pl.when(is_gate)
    def _():
        qkv_ref[...] = jax.nn.sigmoid(_dot(xn_ref[0], w_ref[...])).astype(BF16)

    @pl.when(normed)
    def _():
        acc = _dot(xn_ref[order], w_ref[...])
        hg = hg_ref[...]
        for c in range(acc.shape[1] // HEAD_DIM):
            sl = slice(c * HEAD_DIM, (c + 1) * HEAD_DIM)
            qkv_ref[:, sl] = _rms(acc[:, sl], hg[:, sl]).astype(BF16)

    @pl.when(is_qkv & jnp.logical_not(normed))
    def _():
        qkv_ref[...] = _dot(xn_ref[order], w_ref[...]).astype(BF16)

    @pl.when(j >= BF_TILES)
    def _():
        aux_ref[...] = _dot(xn_ref[0], w_ref[...])


def _inproj(x, gain, w, head_gain, layer):
    t, d = x.shape
    tm, tn = IN_TM, IN_TN
    return pl.pallas_call(
        _inproj_kernel,
        grid=(t // tm, BF_TILES + LAT_TILES),
        in_specs=[
            pl.BlockSpec((tm, d), lambda i, j: (i, 0)),
            pl.BlockSpec((None, 1, d), lambda i, j: (layer, 0, 0)),
            pl.BlockSpec((None, None, d, tn), lambda i, j: (layer, j, 0, 0)),
            pl.BlockSpec((None, 1, tn), lambda i, j: (layer, 0, jnp.clip(j - QKV_TILE0, 0, QKV_TILES - 1))),
        ],
        out_specs=[
            pl.BlockSpec((None, tm, tn), lambda i, j: (jnp.minimum(j, BF_TILES - 1), i, 0)),
            pl.BlockSpec((tm, tn), lambda i, j: (i, jnp.maximum(j - BF_TILES, 0))),
        ],
        out_shape=[jax.ShapeDtypeStruct((BF_TILES, t, tn), BF16), jax.ShapeDtypeStruct((t, LAT_COLS), F32)],
        scratch_shapes=[pltpu.VMEM((d // LANES, tm, LANES), F32), pltpu.VMEM((3, tm, d), BF16)],
        compiler_params=_params("parallel", "arbitrary"),
        name="inproj",
    )(x, gain, w, head_gain)


NA_QROWS = 4
NA_TQ = NA_QROWS * GRID_W
NA_KROWS = 3 * NA_QROWS
NA_TK = NA_KROWS * GRID_W


def _na_bias_tables(rpb):
    heads = rpb.shape[0]
    qc = np.arange(GRID_W)[:, None]
    kc = np.arange(GRID_W)[None, :]
    col_start = np.clip(qc - NA_WIN_C // 2, 0, GRID_W - NA_WIN_C)
    col_ok = (kc >= col_start) & (kc < col_start + NA_WIN_C)
    dc_idx = np.clip(kc - qc, -(NA_WIN_C - 1), NA_WIN_C - 1) + NA_WIN_C - 1
    onehot = (dc_idx[None] == np.arange(2 * NA_WIN_C - 1)[:, None, None]).astype(np.float32)
    by_col = jnp.einsum("hrd,dqk->hqrk", rpb, jnp.asarray(onehot), precision=lax.Precision.HIGHEST)
    by_col = jnp.where(jnp.asarray(col_ok)[None, :, None, :], by_col, NEG_INF)
    tables = []
    for q_rel, k0_rel in ((0, None), (NA_QROWS, 0), (2 * NA_QROWS, None)):
        rows = []
        for j in range(NA_QROWS):
            k0 = j if k0_rel == 0 else (0 if q_rel == 0 else NA_QROWS)
            dr0 = k0 - (q_rel + j) + NA_WIN_R - 1
            win = by_col[:, :, dr0:dr0 + NA_WIN_R, :]
            pad = lambda n: jnp.full((heads, GRID_W, n, GRID_W), NEG_INF, F32)
            rows.append(jnp.concatenate([pad(k0), win, pad(NA_KROWS - NA_WIN_R - k0)], axis=2))
        tables.append(jnp.stack(rows, axis=1).reshape(heads, NA_TQ, NA_TK))
    return jnp.stack(tables)


def _na_kernel(q_ref, k0_ref, k1_ref, k2_ref, v0_ref, v1_ref, v2_ref, bias_ref, o_ref):
    scale = HEAD_DIM ** -0.5
    for h in range(NA_HEADS):
        sl = slice(h * HEAD_DIM, (h + 1) * HEAD_DIM)
        per_tile = IN_TN // HEAD_DIM
        tile, hs = h // per_tile, slice((h % per_tile) * HEAD_DIM, (h % per_tile + 1) * HEAD_DIM)
        q = q_ref[tile, :, hs]
        s = jnp.concatenate([_dot_t(q, k_ref[tile, :, hs]) for k_ref in (k0_ref, k1_ref, k2_ref)], axis=1)
        s = s * scale + bias_ref[h]
        m = jnp.max(s, axis=-1, keepdims=True)
        e = jnp.exp(s - m)
        p = (e * (1.0 / jnp.sum(e, axis=-1, keepdims=True))).astype(BF16)
        o = _dot(p[:, 0:NA_TQ], v0_ref[tile, :, hs])
        o += _dot(p[:, NA_TQ:2 * NA_TQ], v1_ref[tile, :, hs])
        o += _dot(p[:, 2 * NA_TQ:3 * NA_TQ], v2_ref[tile, :, hs])
        o_ref[:, sl] = o.astype(o_ref.dtype)


def _na_attention(qkv, bias, b, s):
    nb = s // NA_TQ
    assert nb >= 3
    width = NA_WIDTH // IN_TN
    q_col = QKV_TILE0 // width
    k_col, v_col = q_col + 1, q_col + 2

    def win(i):
        return jnp.clip(i - 1, 0, nb - 3)

    def kv_spec(col, t):
        return pl.BlockSpec((width, NA_TQ, IN_TN), lambda bi, i: (col, bi * nb + win(i) + t, 0))

    def bias_map(bi, i):
        kind = jnp.where(i == 0, 0, jnp.where(i == nb - 1, 2, 1))
        return (kind, 0, 0, 0)

    return pl.pallas_call(
        _na_kernel,
        grid=(b, nb),
        in_specs=[pl.BlockSpec((width, NA_TQ, IN_TN), lambda bi, i: (q_col, bi * nb + i, 0))]
        + [kv_spec(k_col, t) for t in range(3)]
        + [kv_spec(v_col, t) for t in range(3)]
        + [pl.BlockSpec((None, NA_HEADS, NA_TQ, NA_TK), bias_map)],
        out_specs=pl.BlockSpec((NA_TQ, NA_WIDTH), lambda bi, i: (bi * nb + i, 0)),
        out_shape=jax.ShapeDtypeStruct((b * s, NA_WIDTH), BF16),
        compiler_params=_params("parallel", "arbitrary"),
        name="na_attn",
    )(qkv, qkv, qkv, qkv, qkv, qkv, qkv, bias)


def _rope_tables(s):
    half = MLA_ROPE // 2
    inv = ROPE_THETA ** (-jnp.arange(half, dtype=F32) / half)
    ang = jnp.arange(s).astype(F32)[:, None] * inv[None, :]
    cos, sin = jnp.cos(ang), jnp.sin(ang)
    zero = jnp.zeros((s, LANES - MLA_ROPE), F32)
    return (jnp.concatenate([cos, cos, zero], axis=-1), jnp.concatenate([-sin, sin, zero], axis=-1))


def _mla_prep_kernel(cq_ref, ckv_ref, kr_ref, krr_ref, cos_ref, sin_ref, cqg_ref, ckvg_ref, wuq_ref, wukv_ref,
                     qg_ref, kg_ref, q_out, k_out, v_out):
    cos = cos_ref[...]
    sin = sin_ref[...]
    qf = _dot(_rms(cq_ref[...], cqg_ref[...]).astype(BF16), wuq_ref[...])
    kvf = _dot(_rms(ckv_ref[...], ckvg_ref[...]).astype(BF16), wukv_ref[...])
    kr = kr_ref[...] * cos + krr_ref[...] * sin
    kr_ss = jnp.sum(kr * kr, axis=-1, keepdims=True)
    qg = qg_ref[...]
    kg = kg_ref[...]
    for h in range(MLA_HEADS):
        qb = h * MLA_QEXT
        qn = qf[:, qb:qb + LANES]
        qr = qf[:, qb + LANES:qb + 2 * LANES] * cos + qf[:, qb + 2 * LANES:qb + 3 * LANES] * sin
        ms = (jnp.sum(qn * qn, axis=-1, keepdims=True) + jnp.sum(qr * qr, axis=-1, keepdims=True)) * (1.0 / MLA_QK)
        r = lax.rsqrt(ms + EPS)
        ob = h * MLA_QPAD
        q_out[:, ob:ob + LANES] = (qn * r * qg[:, :LANES]).astype(BF16)
        q_out[:, ob + LANES:ob + 2 * LANES] = (qr * r * qg[:, LANES:]).astype(BF16)
        kb = h * (MLA_NOPE + MLA_V)
        kn = kvf[:, kb:kb + MLA_NOPE]
        msk = (jnp.sum(kn * kn, axis=-1, keepdims=True) + kr_ss) * (1.0 / MLA_QK)
        rk = lax.rsqrt(msk + EPS)
        k_out[:, ob:ob + LANES] = (kn * rk * kg[:, :LANES]).astype(BF16)
        k_out[:, ob + LANES:ob + 2 * LANES] = (kr * rk * kg[:, LANES:]).astype(BF16)
        v_out[:, h * MLA_V:(h + 1) * MLA_V] = kvf[:, kb + MLA_NOPE:kb + MLA_NOPE + MLA_V].astype(BF16)


def _mla_prep(aux, cos, sin, cq_gain, ckv_gain, wuq, wukv, q_gain, k_gain, layer, s, *, tm=256):
    t = aux.shape[0]
    n_pos = s // tm
    full = lambda shape: pl.BlockSpec((None,) + shape, lambda i: (layer,) + (0,) * len(shape))
    return pl.pallas_call(
        _mla_prep_kernel,
        grid=(t // tm,),
        in_specs=[
            pl.BlockSpec((tm, MLA_Q_RANK), lambda i: (i, 0)),
            pl.BlockSpec((tm, MLA_KV_RANK), lambda i: (i, MLA_Q_RANK // MLA_KV_RANK)),
            pl.BlockSpec((tm, LANES), lambda i: (i, (MLA_Q_RANK + MLA_KV_RANK) // LANES)),
            pl.BlockSpec((tm, LANES), lambda i: (i, (MLA_Q_RANK + MLA_KV_RANK) // LANES + 1)),
            pl.BlockSpec((tm, LANES), lambda i: (i % n_pos, 0)),
            pl.BlockSpec((tm, LANES), lambda i: (i % n_pos, 0)),
            full((1, MLA_Q_RANK)),
            full((1, MLA_KV_RANK)),
            full((MLA_Q_RANK, MLA_HEADS * MLA_QEXT)),
            full((MLA_KV_RANK, MLA_HEADS * (MLA_NOPE + MLA_V))),
            full((1, MLA_QPAD)),
            full((1, MLA_QPAD)),
        ],
        out_specs=[
            pl.BlockSpec((tm, MLA_HEADS * MLA_QPAD), lambda i: (i, 0)),
            pl.BlockSpec((tm, MLA_HEADS * MLA_QPAD), lambda i: (i, 0)),
            pl.BlockSpec((tm, MLA_OUT), lambda i: (i, 0)),
        ],
        out_shape=[
            jax.ShapeDtypeStruct((t, MLA_HEADS * MLA_QPAD), BF16),
            jax.ShapeDtypeStruct((t, MLA_HEADS * MLA_QPAD), BF16),
            jax.ShapeDtypeStruct((t, MLA_OUT), BF16),
        ],
        compiler_params=_params("parallel"),
        name="mla_prep",
    )(aux, aux, aux, aux, cos, sin, cq_gain, ckv_gain, wuq, wukv, q_gain, k_gain)


MLA_TQ = 1024
MLA_KCHUNK = 512


def _lane_groups(x, op):
    out = x[:, :LANES]
    for c in range(1, x.shape[1] // LANES):
        out = op(out, x[:, c * LANES:(c + 1) * LANES])
    return out


def _mla_flash_kernel(q_ref, k_ref, v_ref, o_ref, m_ref, l_ref, acc_ref):
    ki = pl.program_id(3)

    @pl.when(ki == 0)
    def _():
        m_ref[...] = jnp.full_like(m_ref, -jnp.inf)
        l_ref[...] = jnp.zeros_like(l_ref)
        acc_ref[...] = jnp.zeros_like(acc_ref)

    q = q_ref[...]
    c2 = (MLA_QK ** -0.5) * LOG2E
    m, l, acc = m_ref[...], l_ref[...], acc_ref[...]
    for c in range(k_ref.shape[0] // MLA_KCHUNK):
        ks = slice(c * MLA_KCHUNK, (c + 1) * MLA_KCHUNK)
        s = _dot_t(q, k_ref[ks, :]) * c2
        m_new = jnp.maximum(m, jnp.max(_lane_groups(s, jnp.maximum), axis=-1, keepdims=True))
        alpha = jnp.exp2(m - m_new)
        p = jnp.exp2(s - jnp.concatenate([m_new] * (MLA_KCHUNK // LANES), axis=1))
        l = alpha * l + _lane_groups(p, jnp.add)
        acc = alpha * acc + _dot(p.astype(BF16), v_ref[ks, :])
        m = m_new
    m_ref[...] = m
    l_ref[...] = l
    acc_ref[...] = acc

    @pl.when(ki == pl.num_programs(3) - 1)
    def _():
        o_ref[...] = (acc * (1.0 / jnp.sum(l, axis=-1, keepdims=True))).astype(o_ref.dtype)


def _mla_attention(q, k, v, b, s, *, tk=8192):
    tq, tk = MLA_TQ, min(tk, s)
    nq, nk = s // tq, s // tk
    stat = pltpu.VMEM((tq, LANES), F32)
    return pl.pallas_call(
        _mla_flash_kernel,
        grid=(b, MLA_HEADS, nq, nk),
        in_specs=[
            pl.BlockSpec((tq, MLA_QPAD), lambda bi, h, qi, ki: (bi * nq + qi, h)),
            pl.BlockSpec((tk, MLA_QPAD), lambda bi, h, qi, ki: (bi * nk + ki, h)),
            pl.BlockSpec((tk, MLA_V), lambda bi, h, qi, ki: (bi * nk + ki, h)),
        ],
        out_specs=pl.BlockSpec((tq, MLA_V), lambda bi, h, qi, ki: (bi * nq + qi, h)),
        out_shape=jax.ShapeDtypeStruct((b * s, MLA_OUT), BF16),
        scratch_shapes=[stat, stat, pltpu.VMEM((tq, MLA_V), F32)],
        compiler_params=_params("parallel", "parallel", "parallel", "arbitrary"),
        name="mla_flash",
    )(q, k, v)


DIL_TQ = 256
DIL_KBLOCKS = 4


def _alibi_slopes(n):
    return 2.0 ** (-8.0 * jnp.arange(1, n + 1, dtype=F32) / n)


def _dil_kernel(slope_ref, q_ref, *refs, n, radius, tq):
    k_refs, v_refs = refs[:DIL_KBLOCKS], refs[DIL_KBLOCKS:2 * DIL_KBLOCKS]
    o_ref, lse_ref = refs[2 * DIL_KBLOCKS:]
    kb = tq // 2
    tk = DIL_KBLOCKS * kb
    i = pl.program_id(2)
    row = lax.broadcasted_iota(jnp.int32, (tq, tk), 0)
    col = lax.broadcasted_iota(jnp.int32, (tq, tk), 1)
    kpos = (2 * i - 1) * kb + col
    dist_i = jnp.abs(row + kb - col)
    valid = (kpos >= 0) & (kpos < n) & (dist_i <= radius)
    dist = dist_i.astype(F32)
    scale = HEAD_DIM ** -0.5
    for h in range(DIL_HEADS_PER_GROUP):
        sl = slice(h * HEAD_DIM, (h + 1) * HEAD_DIM)
        q = q_ref[..., sl].reshape(tq, HEAD_DIM)
        k = jnp.concatenate([ref[..., sl].reshape(kb, HEAD_DIM) for ref in k_refs], axis=0)
        v = jnp.concatenate([ref[..., sl].reshape(kb, HEAD_DIM) for ref in v_refs], axis=0)
        s = _dot_t(q, k) * scale - slope_ref[h] * dist
        s = jnp.where(valid, s, NEG_INF)
        m = jnp.max(s, axis=-1, keepdims=True)
        p = jnp.exp(s - m)
        den = jnp.sum(p, axis=-1, keepdims=True)
        o = _dot(p.astype(BF16), v) * (1.0 / den)
        o_ref[..., sl] = o.reshape(o_ref.shape[:-1] + (HEAD_DIM,))
        lse_ref[..., sl] = jnp.broadcast_to(m + jnp.log(den), (tq, HEAD_DIM)).reshape(o_ref.shape[:-1] + (HEAD_DIM,))


def _dil_group_attention(qkv, slopes, group, b, s):
    window, dil = DIL_GROUPS[group]
    radius = (window // 2) // dil
    n = s // dil
    tq = min(DIL_TQ, n)
    kb = tq // 2
    nblk = n // tq
    assert radius <= kb and s % IN_TM == 0 and n % tq == 0
    t = b * s
    rows = IN_TM // dil
    tiles = s // IN_TM
    q_unit = QKV_TILE0 + NA_TILES + 3 * group

    def spec(br, unit, pick):
        if rows >= br:
            sub = rows // br
            block = (None, None, br, DIL_OUT)
            index = lambda bi, r, u: (bi * tiles + u // sub, r, u % sub, 0)
        else:
            span = br // rows
            assert tiles % span == 0
            block = (span, None, rows, DIL_OUT)
            index = lambda bi, r, u: (bi * tiles // span + u, r, 0, 0)
        if unit is None:
            return pl.BlockSpec(block, lambda bi, r, i: index(bi, r, pick(i)))
        return pl.BlockSpec((None,) + block, lambda bi, r, i: (unit,) + index(bi, r, pick(i)))

    def kv_specs(unit):
        return [spec(kb, unit, lambda i, j=j: jnp.clip(2 * i - 1 + j, 0, n // kb - 1)) for j in range(DIL_KBLOCKS)]

    qkv_v = qkv.reshape(BF_TILES, t // IN_TM, dil, rows, IN_TN)
    kern = functools.partial(_dil_kernel, n=n, radius=radius, tq=tq)
    out_spec = spec(tq, None, lambda i: i)
    o, lse = pl.pallas_call(
        kern,
        grid=(b, dil, nblk),
        in_specs=[pl.BlockSpec(memory_space=pltpu.SMEM), spec(tq, q_unit, lambda i: i)]
        + kv_specs(q_unit + 1) + kv_specs(q_unit + 2),
        out_specs=[out_spec, out_spec],
        out_shape=[jax.ShapeDtypeStruct((t // IN_TM, dil, rows, DIL_OUT), F32)] * 2,
        compiler_params=_params("parallel", "parallel", "arbitrary"),
        name=f"dil_attn_g{group}",
    )(slopes, *([qkv_v] * (1 + 2 * DIL_KBLOCKS)))
    return o, lse


MERGE_TM = 512
NCH = DIL_OUT // LANES


def _merge_kernel(ona_ref, omla_ref, od0_ref, od1_ref, od2_ref, ls0_ref, ls1_ref, ls2_ref,
                  gna_ref, gmla_ref, gdil_ref, wna_ref, wmla_ref, wdil_ref, o_ref, od_ref, tok_ref):
    j = pl.program_id(1)
    tm = MERGE_TM

    @pl.when(j == 0)
    def _():
        for a, (ref, dil) in enumerate(((od1_ref, 4), (ls1_ref, 4), (od2_ref, 16), (ls2_ref, 16))):
            for r in range(dil):
                for c in range(NCH):
                    tok_ref[a, c, pl.ds(r, tm // dil, stride=dil), :] = ref[r, :, c * LANES:(c + 1) * LANES]
        for c in range(NCH):
            cs = slice(c * LANES, (c + 1) * LANES)
            od0, ls0 = od0_ref[0, :, cs], ls0_ref[0, :, cs]
            od1, ls1, od2, ls2 = tok_ref[0, c], tok_ref[1, c], tok_ref[2, c], tok_ref[3, c]
            top = jnp.maximum(jnp.maximum(ls0, ls1), ls2)
            w0, w1, w2 = jnp.exp(ls0 - top), jnp.exp(ls1 - top), jnp.exp(ls2 - top)
            inv = 1.0 / (w0 + w1 + w2)
            od = (w0 * inv) * od0 + (w1 * inv) * od1 + (w2 * inv) * od2
            od_ref[:, cs] = od.astype(BF16)

    gate = lambda ref: ref[...].astype(F32)
    merged = gate(gna_ref) * _dot(ona_ref[...], wna_ref[j])
    merged += gate(gmla_ref) * _dot(omla_ref[...], wmla_ref[j])
    merged += gate(gdil_ref) * _dot(od_ref[...], wdil_ref[j])
    o_ref[...] = merged.astype(o_ref.dtype)


def _col_tiles(w, tn):
    depth, k, n = w.shape
    return w.reshape(depth, k, n // tn, tn).transpose(0, 2, 1, 3)


def _resident(w, layer):
    return pl.BlockSpec((None,) + w.shape[1:], lambda i, j: (layer, 0, 0, 0))


def _merge(o_na, o_mla, o_dil, lse_dil, gates, w_na, w_mla, w_dil, layer):
    t = o_na.shape[0]
    tm = MERGE_TM
    tn = w_na.shape[-1]
    halves = IN_TM // tm
    gstep = D_MODEL // tn
    row = lambda width: pl.BlockSpec((tm, width), lambda i, j: (i, 0))
    gate = lambda g: pl.BlockSpec((None, tm, tn), lambda i, j: (g * gstep + j, i, 0))

    def dil_spec(g):
        dil = DIL_GROUPS[g][1]
        return pl.BlockSpec((None, dil, tm // dil, DIL_OUT), lambda i, j: (i // halves, 0, i % halves, 0))

    return pl.pallas_call(
        _merge_kernel,
        grid=(t // tm, D_MODEL // tn),
        in_specs=[row(NA_WIDTH), row(MLA_OUT)] + [dil_spec(g) for g in range(3)] * 2
        + [gate(0), gate(1), gate(2)] + [_resident(w, layer) for w in (w_na, w_mla, w_dil)],
        out_specs=pl.BlockSpec((tm, tn), lambda i, j: (i, j)),
        out_shape=jax.ShapeDtypeStruct((t, D_MODEL), BF16),
        scratch_shapes=[pltpu.VMEM((tm, DIL_OUT), BF16), pltpu.VMEM((4, NCH, tm, LANES), F32)],
        compiler_params=_params("parallel", "arbitrary"),
        name="merge",
    )(o_na, o_mla, *o_dil, *lse_dil, gates, gates, gates, w_na, w_mla, w_dil)


def _resid_matmul_kernel(a_ref, w_ref, x_ref, o_ref):
    o_ref[...] = x_ref[...] + _dot(a_ref[...], w_ref[pl.program_id(1)])


def _resid_matmul(a, w, x, layer, *, tm=1024):
    t, k = a.shape
    tn = w.shape[-1]
    n = w.shape[1] * tn
    return pl.pallas_call(
        _resid_matmul_kernel,
        grid=(t // tm, n // tn),
        in_specs=[
            pl.BlockSpec((tm, k), lambda i, j: (i, 0)),
            _resident(w, layer),
            pl.BlockSpec((tm, tn), lambda i, j: (i, j)),
        ],
        out_specs=pl.BlockSpec((tm, tn), lambda i, j: (i, j)),
        out_shape=jax.ShapeDtypeStruct((t, n), F32),
        compiler_params=_params("parallel", "arbitrary"),
        name="wo_resid",
    )(a, w, x)


def _prepare(p):
    depth = p["w_in"].shape[0]
    row = lambda g: g[:, None, :].astype(F32)
    w_in = p["w_in"]
    o = np.cumsum((0, NA_WIDTH, NA_WIDTH, NA_WIDTH, MLA_Q_RANK, MLA_KV_RANK, MLA_ROPE,
                   DIL_WIDTH, DIL_WIDTH, DIL_WIDTH, D_MODEL, D_MODEL, D_MODEL))
    seg = lambda a: w_in[:, :, o[a]:o[a + 1]]
    grp = lambda a, g: seg(a)[:, :, g * DIL_OUT:(g + 1) * DIL_OUT]
    rot = np.concatenate([np.arange(MLA_ROPE // 2, MLA_ROPE), np.arange(MLA_ROPE // 2)])
    zpad = jnp.zeros((depth, D_MODEL, LANES - MLA_ROPE), w_in.dtype)
    cols = [seg(9), seg(10), seg(11), seg(0), seg(1), seg(2)]
    for g in range(len(DIL_GROUPS)):
        cols += [grp(6, g), grp(7, g), grp(8, g)]
    cols += [seg(3), seg(4), seg(5), zpad, seg(5)[:, :, rot], zpad]
    w_all = _col_tiles(jnp.concatenate(cols, axis=-1).astype(BF16), IN_TN)

    tile = lambda g, reps: jnp.tile(g, (1, reps))
    ones = lambda n: jnp.ones((depth, n), F32)
    dil_gain = jnp.concatenate([tile(p["dil_q_norm"], DIL_HEADS_PER_GROUP), tile(p["dil_k_norm"], DIL_HEADS_PER_GROUP),
                                ones(DIL_OUT)], axis=-1)
    head_gain = jnp.concatenate(
        [tile(p["na_q_norm"], NA_HEADS), tile(p["na_k_norm"], NA_HEADS), ones(NA_WIDTH)]
        + [dil_gain] * len(DIL_GROUPS), axis=-1)

    wuq = p["mla_w_uq"].reshape(depth, MLA_Q_RANK, MLA_HEADS, MLA_QK)
    zq = jnp.zeros((depth, MLA_Q_RANK, MLA_HEADS, LANES - MLA_ROPE), wuq.dtype)
    wuq_rope = wuq[..., MLA_NOPE:]
    wuq = jnp.concatenate([wuq[..., :MLA_NOPE], wuq_rope, zq, wuq_rope[..., rot], zq], axis=-1)
    pad_gain = lambda g: jnp.concatenate([g, jnp.zeros((depth, MLA_QPAD - MLA_QK), F32)], axis=-1)

    return dict(
        ffn1=(row(p["ffn1_norm"]), _col_tiles(p["ffn1_w_gate"].astype(BF16), FFN_TF),
              _col_tiles(p["ffn1_w_up"].astype(BF16), FFN_TF), p["ffn1_w_down"].astype(BF16)),
        ffn2=(row(p["ffn2_norm"]), _col_tiles(p["ffn2_w_gate"].astype(BF16), FFN_TF),
              _col_tiles(p["ffn2_w_up"].astype(BF16), FFN_TF), p["ffn2_w_down"].astype(BF16)),
        mix_norm=row(p["mix_norm"]),
        w_all=w_all, head_gain=row(head_gain),
        na_bias=[_na_bias_tables(p["na_rpb"][l]) for l in range(depth)],
        cq_gain=row(p["mla_cq_norm"]), ckv_gain=row(p["mla_ckv_norm"]),
        wuq=wuq.reshape(depth, MLA_Q_RANK, MLA_HEADS * MLA_QEXT).astype(BF16),
        wukv=p["mla_w_ukv"].astype(BF16),
        q_gain=row(pad_gain(p["mla_q_norm"])), k_gain=row(pad_gain(p["mla_k_norm"])),
        w_na=_col_tiles(p["w_na_out"].astype(BF16), IN_TN), w_mla=_col_tiles(p["w_mla_out"].astype(BF16), IN_TN),
        w_dil=_col_tiles(p["w_dil_out"].astype(BF16), IN_TN), w_o=_col_tiles(p["w_o"].astype(BF16), IN_TN),
    )


def _dil_slopes():
    slopes = _alibi_slopes(DIL_HEADS)
    out = []
    for g, (_, dil) in enumerate(DIL_GROUPS):
        lo, hi = g * DIL_HEADS_PER_GROUP, (g + 1) * DIL_HEADS_PER_GROUP
        out.append(slopes[lo:hi] * dil)
    return out


def _layer(x, w, layer, b, s, rope, slopes):
    x = _ffn(x, *w["ffn1"], layer)
    qkv, aux = _inproj(x, w["mix_norm"], w["w_all"], w["head_gain"], layer)
    o_na = _na_attention(qkv, w["na_bias"][layer], b, s)
    q, k, v = _mla_prep(aux, rope[0], rope[1], w["cq_gain"], w["ckv_gain"], w["wuq"], w["wukv"],
                        w["q_gain"], w["k_gain"], layer, s)
    o_mla = _mla_attention(q, k, v, b, s)
    dil = [_dil_group_attention(qkv, slopes[g], g, b, s) for g in range(len(DIL_GROUPS))]
    merged = _merge(o_na, o_mla, [d[0] for d in dil], [d[1] for d in dil], qkv,
                    w["w_na"], w["w_mla"], w["w_dil"], layer)
    x = _resid_matmul(merged, w["w_o"], x, layer)
    return _ffn(x, *w["ffn2"], layer)


def _trunk(x, w, slopes):
    b, s, d = x.shape
    rope = _rope_tables(s)
    y = x.reshape(b * s, d)
    for layer in range(w["w_all"].shape[0]):
        y = _layer(y, w, layer, b, s, rope, slopes)
    return y.reshape(b, s, d)


def kernel(x_prompt, x_sample, ffn1_norm, ffn1_w_gate, ffn1_w_up, ffn1_w_down, mix_norm, w_in, na_q_norm, na_k_norm, na_rpb, mla_cq_norm, mla_ckv_norm, mla_w_uq, mla_w_ukv, mla_q_norm, mla_k_norm, dil_q_norm, dil_k_norm, w_na_out, w_mla_out, w_dil_out, w_o, ffn2_norm, ffn2_w_gate, ffn2_w_up, ffn2_w_down):
    w = _prepare(dict(
        ffn1_norm=ffn1_norm, ffn1_w_gate=ffn1_w_gate, ffn1_w_up=ffn1_w_up, ffn1_w_down=ffn1_w_down,
        mix_norm=mix_norm, w_in=w_in, na_q_norm=na_q_norm, na_k_norm=na_k_norm, na_rpb=na_rpb,
        mla_cq_norm=mla_cq_norm, mla_ckv_norm=mla_ckv_norm, mla_w_uq=mla_w_uq, mla_w_ukv=mla_w_ukv,
        mla_q_norm=mla_q_norm, mla_k_norm=mla_k_norm, dil_q_norm=dil_q_norm, dil_k_norm=dil_k_norm,
        w_na_out=w_na_out, w_mla_out=w_mla_out, w_dil_out=w_dil_out, w_o=w_o,
        ffn2_norm=ffn2_norm, ffn2_w_gate=ffn2_w_gate, ffn2_w_up=ffn2_w_up, ffn2_w_down=ffn2_w_down))
    slopes = _dil_slopes()
    return (_trunk(x_prompt, w, slopes), _trunk(x_sample, w, slopes))
```

```python
import functools

import numpy as np
import jax
import jax.numpy as jnp
from jax import lax
from jax.experimental import pallas as pl
from jax.experimental.pallas import tpu as pltpu

F32 = jnp.float32
BF16 = jnp.bfloat16

D_MODEL = 2048
GRID_W = 64
HEAD_DIM = 128
EPS = 1e-6
NEG_INF = -1e30
LOG2E = 1.4426950408889634

NA_HEADS = 8
NA_WIN_R = 8
NA_WIN_C = 16
NA_WIDTH = NA_HEADS * HEAD_DIM

MLA_HEADS = 8
MLA_Q_RANK = 512
MLA_KV_RANK = 256
MLA_NOPE = 128
MLA_ROPE = 64
MLA_V = 128
MLA_QK = MLA_NOPE + MLA_ROPE
MLA_OUT = MLA_HEADS * MLA_V
ROPE_THETA = 10000.0

DIL_GROUPS = ((128, 1), (512, 4), (2048, 16))
DIL_HEADS_PER_GROUP = 4
DIL_HEADS = 12
DIL_WIDTH = DIL_HEADS * HEAD_DIM
DIL_OUT = DIL_HEADS_PER_GROUP * HEAD_DIM

D_FF = 5632
FFN_TF = 512

LANES = 128
VMEM_LIMIT = 56 * 1024 * 1024

IN_TM = 1024
IN_TN = 512
GATE_TILES = 3 * D_MODEL // IN_TN
QKV_TILE0 = GATE_TILES
QKV_TILES = (3 * NA_WIDTH + 3 * DIL_WIDTH) // IN_TN
NA_TILES = 3 * NA_WIDTH // IN_TN
BF_TILES = GATE_TILES + QKV_TILES
BF_COLS = BF_TILES * IN_TN
LAT_COLS = MLA_Q_RANK + MLA_KV_RANK + 2 * LANES
LAT_TILES = LAT_COLS // IN_TN
MLA_QPAD = 2 * LANES
MLA_QEXT = 3 * LANES


def _params(*sem):
    return pltpu.CompilerParams(dimension_semantics=sem, vmem_limit_bytes=VMEM_LIMIT)


def _rms(x, gain):
    return x * lax.rsqrt(jnp.mean(x * x, axis=-1, keepdims=True) + EPS) * gain


def _dot(a, b):
    return jnp.dot(a, b, preferred_element_type=F32)


def _dot_t(a, b):
    return lax.dot_general(a, b, (((1,), (1,)), ((), ())), preferred_element_type=F32)


def _ffn_kernel(x_ref, g_ref, wg_ref, wu_ref, wd_ref, o_ref, xn_ref):
    f = pl.program_id(1)

    @pl.when(f == 0)
    def _():
        x = x_ref[...]
        xn_ref[...] = _rms(x, g_ref[...]).astype(BF16)
        o_ref[...] = x

    xn = xn_ref[...]
    g = _dot(xn, wg_ref[...])
    u = _dot(xn, wu_ref[...])
    h = (g * jax.nn.sigmoid(g) * u * 0.5).astype(BF16)
    o_ref[...] += _dot(h, wd_ref[...])


def _ffn(x, gain, wg, wu, wd, layer, *, tm=1024, tf=FFN_TF):
    t, d = x.shape
    ff = wg.shape[-1]
    return pl.pallas_call(
        _ffn_kernel,
        grid=(t // tm, ff // tf),
        in_specs=[
            pl.BlockSpec((tm, d), lambda i, f: (i, 0)),
            pl.BlockSpec((None, 1, d), lambda i, f: (layer, 0, 0)),
            pl.BlockSpec((None, d, tf), lambda i, f: (layer, 0, f)),
            pl.BlockSpec((None, d, tf), lambda i, f: (layer, 0, f)),
            pl.BlockSpec((None, tf, d), lambda i, f: (layer, f, 0)),
        ],
        out_specs=pl.BlockSpec((tm, d), lambda i, f: (i, 0)),
        out_shape=jax.ShapeDtypeStruct((t, d), F32),
        scratch_shapes=[pltpu.VMEM((tm, d), BF16)],
        compiler_params=_params("parallel", "arbitrary"),
        name="ffn",
    )(x, gain, wg, wu, wd)


def _inproj_kernel(x_ref, g_ref, w_ref, hg_ref, qkv_ref, aux_ref, yn_ref, xn_ref):
    j = pl.program_id(1)
    tm, d = x_ref.shape

    @pl.when(j == 0)
    def _():
        x = x_ref[...]
        inv = lax.rsqrt(jnp.mean(x * x, axis=-1, keepdims=True) + EPS)
        for c in range(d // LANES):
            cs = slice(c * LANES, (c + 1) * LANES)
            y = x_ref[:, cs] * inv * g_ref[:, cs]
            yn_ref[c] = y
            xn_ref[0, :, cs] = y.astype(BF16)
        for g in (1, 2):
            dil = DIL_GROUPS[g][1]
            rows = tm // dil
            for r in range(dil):
                for c in range(d // LANES):
                    xn_ref[g, r * rows:(r + 1) * rows, c * LANES:(c + 1) * LANES] = (
                        yn_ref[c, pl.ds(r, rows, stride=dil), :].astype(BF16))

    jq = j - QKV_TILE0
    is_gate = j < QKV_TILE0
    is_qkv = (jq >= 0) & (jq < QKV_TILES)
    normed = is_qkv & ((jq < 4) | ((jq >= NA_TILES) & ((jq - NA_TILES) % 3 != 2)))
    order = jnp.where(jq < NA_TILES + 3, 0, jnp.where(jq < NA_TILES + 6, 1, 2))

    @pl.when(is_gate)
    def _():
        qkv_ref[...] = jax.nn.sigmoid(_dot(xn_ref[0], w_ref[...])).astype(BF16)

    @pl.when(normed)
    def _():
        acc = _dot(xn_ref[order], w_ref[...])
        hg = hg_ref[...]
        for c in range(acc.shape[1] // HEAD_DIM):
            sl = slice(c * HEAD_DIM, (c + 1) * HEAD_DIM)
            qkv_ref[:, sl] = _rms(acc[:, sl], hg[:, sl]).astype(BF16)

    @pl.when(is_qkv & jnp.logical_not(normed))
    def _():
        qkv_ref[...] = _dot(xn_ref[order], w_ref[...]).astype(BF16)

    @pl.when(j >= BF_TILES)
    def _():
        aux_ref[...] = _dot(xn_ref[0], w_ref[...])


def _inproj(x, gain, w, head_gain, layer):
    t, d = x.shape
    tm, tn = IN_TM, IN_TN
    return pl.pallas_call(
        _inproj_kernel,
        grid=(t // tm, BF_TILES + LAT_TILES),
        in_specs=[
            pl.BlockSpec((tm, d), lambda i, j: (i, 0)),
            pl.BlockSpec((None, 1, d), lambda i, j: (layer, 0, 0)),
            pl.BlockSpec((None, d, tn), lambda i, j: (layer, 0, j)),
            pl.BlockSpec((None, 1, tn), lambda i, j: (layer, 0, jnp.clip(j - QKV_TILE0, 0, QKV_TILES - 1))),
        ],
        out_specs=[
            pl.BlockSpec((None, tm, tn), lambda i, j: (jnp.minimum(j, BF_TILES - 1), i, 0)),
            pl.BlockSpec((tm, tn), lambda i, j: (i, jnp.maximum(j - BF_TILES, 0))),
        ],
        out_shape=[jax.ShapeDtypeStruct((BF_TILES, t, tn), BF16), jax.ShapeDtypeStruct((t, LAT_COLS), F32)],
        scratch_shapes=[pltpu.VMEM((d // LANES, tm, LANES), F32), pltpu.VMEM((3, tm, d), BF16)],
        compiler_params=_params("parallel", "arbitrary"),
        name="inproj",
    )(x, gain, w, head_gain)


NA_QROWS = 4
NA_TQ = NA_QROWS * GRID_W
NA_KROWS = 3 * NA_QROWS
NA_TK = NA_KROWS * GRID_W


def _na_bias_tables(rpb):
    heads = rpb.shape[0]
    qc = np.arange(GRID_W)[:, None]
    kc = np.arange(GRID_W)[None, :]
    col_start = np.clip(qc - NA_WIN_C // 2, 0, GRID_W - NA_WIN_C)
    col_ok = (kc >= col_start) & (kc < col_start + NA_WIN_C)
    dc_idx = np.clip(kc - qc, -(NA_WIN_C - 1), NA_WIN_C - 1) + NA_WIN_C - 1
    onehot = (dc_idx[None] == np.arange(2 * NA_WIN_C - 1)[:, None, None]).astype(np.float32)
    by_col = jnp.einsum("hrd,dqk->hqrk", rpb, jnp.asarray(onehot), precision=lax.Precision.HIGHEST)
    by_col = jnp.where(jnp.asarray(col_ok)[None, :, None, :], by_col, NEG_INF)
    tables = []
    for q_rel, k0_rel in ((0, None), (NA_QROWS, 0), (2 * NA_QROWS, None)):
        rows = []
        for j in range(NA_QROWS):
            k0 = j if k0_rel == 0 else (0 if q_rel == 0 else NA_QROWS)
            dr0 = k0 - (q_rel + j) + NA_WIN_R - 1
            win = by_col[:, :, dr0:dr0 + NA_WIN_R, :]
            pad = lambda n: jnp.full((heads, GRID_W, n, GRID_W), NEG_INF, F32)
            rows.append(jnp.concatenate([pad(k0), win, pad(NA_KROWS - NA_WIN_R - k0)], axis=2))
        tables.append(jnp.stack(rows, axis=1).reshape(heads, NA_TQ, NA_TK))
    return jnp.stack(tables)


def _head_cols(h):
    per_tile = IN_TN // HEAD_DIM
    return h // per_tile, slice((h % per_tile) * HEAD_DIM, (h % per_tile + 1) * HEAD_DIM)


def _na_kernel(q_ref, k0_ref, k1_ref, k2_ref, v0_ref, v1_ref, v2_ref, bias_ref, o_ref):
    scale = HEAD_DIM ** -0.5

    def window(refs, h):
        tile, hs = _head_cols(h)
        return jnp.concatenate([ref[tile, :, hs] for ref in refs], axis=0)

    def scores(h):
        tile, hs = _head_cols(h)
        return _dot_t(q_ref[tile, :, hs], window((k0_ref, k1_ref, k2_ref), h))

    s_next = scores(0)
    for h in range(NA_HEADS):
        s = s_next
        if h + 1 < NA_HEADS:
            s_next = scores(h + 1)
        s = s * scale + bias_ref[h]
        m = jnp.max(s, axis=-1, keepdims=True)
        e = jnp.exp(s - m)
        den = jnp.sum(e, axis=-1, keepdims=True)
        o = _dot(e.astype(BF16), window((v0_ref, v1_ref, v2_ref), h)) * (1.0 / den)
        o_ref[:, h * HEAD_DIM:(h + 1) * HEAD_DIM] = o.astype(o_ref.dtype)


def _na_attention(qkv, bias, b, s):
    nb = s // NA_TQ
    assert nb >= 3
    width = NA_WIDTH // IN_TN
    q_col = QKV_TILE0 // width
    k_col, v_col = q_col + 1, q_col + 2

    def win(i):
        return jnp.clip(i - 1, 0, nb - 3)

    def kv_spec(col, t):
        return pl.BlockSpec((width, NA_TQ, IN_TN), lambda bi, i: (col, bi * nb + win(i) + t, 0))

    def bias_map(bi, i):
        kind = jnp.where(i == 0, 0, jnp.where(i == nb - 1, 2, 1))
        return (kind, 0, 0, 0)

    return pl.pallas_call(
        _na_kernel,
        grid=(b, nb),
        in_specs=[pl.BlockSpec((width, NA_TQ, IN_TN), lambda bi, i: (q_col, bi * nb + i, 0))]
        + [kv_spec(k_col, t) for t in range(3)]
        + [kv_spec(v_col, t) for t in range(3)]
        + [pl.BlockSpec((None, NA_HEADS, NA_TQ, NA_TK), bias_map)],
        out_specs=pl.BlockSpec((NA_TQ, NA_WIDTH), lambda bi, i: (bi * nb + i, 0)),
        out_shape=jax.ShapeDtypeStruct((b * s, NA_WIDTH), BF16),
        compiler_params=_params("parallel", "arbitrary"),
        name="na_attn",
    )(qkv, qkv, qkv, qkv, qkv, qkv, qkv, bias)


def _rope_tables(s):
    half = MLA_ROPE // 2
    inv = ROPE_THETA ** (-jnp.arange(half, dtype=F32) / half)
    ang = jnp.arange(s).astype(F32)[:, None] * inv[None, :]
    cos, sin = jnp.cos(ang), jnp.sin(ang)
    zero = jnp.zeros((s, LANES - MLA_ROPE), F32)
    return (jnp.concatenate([cos, cos, zero], axis=-1), jnp.concatenate([-sin, sin, zero], axis=-1))


def _mla_prep_kernel(cq_ref, ckv_ref, kr_ref, krr_ref, cos_ref, sin_ref, cqg_ref, ckvg_ref, wuq_ref, wukv_ref,
                     qg_ref, kg_ref, q_out, k_out, v_out):
    cos = cos_ref[...]
    sin = sin_ref[...]
    qf = _dot(_rms(cq_ref[...], cqg_ref[...]).astype(BF16), wuq_ref[...])
    kvf = _dot(_rms(ckv_ref[...], ckvg_ref[...]).astype(BF16), wukv_ref[...])
    kr = kr_ref[...] * cos + krr_ref[...] * sin
    kr_ss = jnp.sum(kr * kr, axis=-1, keepdims=True)
    qg = qg_ref[...]
    kg = kg_ref[...]
    for h in range(MLA_HEADS):
        qb = h * MLA_QEXT
        qn = qf[:, qb:qb + LANES]
        qr = qf[:, qb + LANES:qb + 2 * LANES] * cos + qf[:, qb + 2 * LANES:qb + 3 * LANES] * sin
        ms = (jnp.sum(qn * qn, axis=-1, keepdims=True) + jnp.sum(qr * qr, axis=-1, keepdims=True)) * (1.0 / MLA_QK)
        r = lax.rsqrt(ms + EPS)
        ob = h * MLA_QPAD
        q_out[:, ob:ob + LANES] = (qn * r * qg[:, :LANES]).astype(BF16)
        q_out[:, ob + LANES:ob + 2 * LANES] = (qr * r * qg[:, LANES:]).astype(BF16)
        kb = h * (MLA_NOPE + MLA_V)
        kn = kvf[:, kb:kb + MLA_NOPE]
        msk = (jnp.sum(kn * kn, axis=-1, keepdims=True) + kr_ss) * (1.0 / MLA_QK)
        rk = lax.rsqrt(msk + EPS)
        k_out[:, ob:ob + LANES] = (kn * rk * kg[:, :LANES]).astype(BF16)
        k_out[:, ob + LANES:ob + 2 * LANES] = (kr * rk * kg[:, LANES:]).astype(BF16)
        v_out[:, h * MLA_V:(h + 1) * MLA_V] = kvf[:, kb + MLA_NOPE:kb + MLA_NOPE + MLA_V].astype(BF16)


def _mla_prep(aux, cos, sin, cq_gain, ckv_gain, wuq, wukv, q_gain, k_gain, layer, s, *, tm=256):
    t = aux.shape[0]
    n_pos = s // tm
    full = lambda shape: pl.BlockSpec((None,) + shape, lambda i: (layer,) + (0,) * len(shape))
    return pl.pallas_call(
        _mla_prep_kernel,
        grid=(t // tm,),
        in_specs=[
            pl.BlockSpec((tm, MLA_Q_RANK), lambda i: (i, 0)),
            pl.BlockSpec((tm, MLA_KV_RANK), lambda i: (i, MLA_Q_RANK // MLA_KV_RANK)),
            pl.BlockSpec((tm, LANES), lambda i: (i, (MLA_Q_RANK + MLA_KV_RANK) // LANES)),
            pl.BlockSpec((tm, LANES), lambda i: (i, (MLA_Q_RANK + MLA_KV_RANK) // LANES + 1)),
            pl.BlockSpec((tm, LANES), lambda i: (i % n_pos, 0)),
            pl.BlockSpec((tm, LANES), lambda i: (i % n_pos, 0)),
            full((1, MLA_Q_RANK)),
            full((1, MLA_KV_RANK)),
            full((MLA_Q_RANK, MLA_HEADS * MLA_QEXT)),
            full((MLA_KV_RANK, MLA_HEADS * (MLA_NOPE + MLA_V))),
            full((1, MLA_QPAD)),
            full((1, MLA_QPAD)),
        ],
        out_specs=[
            pl.BlockSpec((tm, MLA_HEADS * MLA_QPAD), lambda i: (i, 0)),
            pl.BlockSpec((tm, MLA_HEADS * MLA_QPAD), lambda i: (i, 0)),
            pl.BlockSpec((tm, MLA_OUT), lambda i: (i, 0)),
        ],
        out_shape=[
            jax.ShapeDtypeStruct((t, MLA_HEADS * MLA_QPAD), BF16),
            jax.ShapeDtypeStruct((t, MLA_HEADS * MLA_QPAD), BF16),
            jax.ShapeDtypeStruct((t, MLA_OUT), BF16),
        ],
        compiler_params=_params("parallel"),
        name="mla_prep",
    )(aux, aux, aux, aux, cos, sin, cq_gain, ckv_gain, wuq, wukv, q_gain, k_gain)


MLA_TQ = 1024
MLA_KCHUNK = 512


def _lane_groups(x, op):
    out = x[:, :LANES]
    for c in range(1, x.shape[1] // LANES):
        out = op(out, x[:, c * LANES:(c + 1) * LANES])
    return out


def _mla_flash_kernel(q_ref, k_ref, v_ref, o_ref, m_ref, l_ref, acc_ref):
    ki = pl.program_id(3)

    @pl.when(ki == 0)
    def _():
        m_ref[...] = jnp.full_like(m_ref, -jnp.inf)
        l_ref[...] = jnp.zeros_like(l_ref)
        acc_ref[...] = jnp.zeros_like(acc_ref)

    q = q_ref[...]
    c2 = (MLA_QK ** -0.5) * LOG2E
    m, l, acc = m_ref[...], l_ref[...], acc_ref[...]
    for c in range(k_ref.shape[0] // MLA_KCHUNK):
        ks = slice(c * MLA_KCHUNK, (c + 1) * MLA_KCHUNK)
        s = _dot_t(q, k_ref[ks, :]) * c2
        m_new = jnp.maximum(m, jnp.max(_lane_groups(s, jnp.maximum), axis=-1, keepdims=True))
        alpha = jnp.exp2(m - m_new)
        p = jnp.exp2(s - jnp.concatenate([m_new] * (MLA_KCHUNK // LANES), axis=1))
        l = alpha * l + _lane_groups(p, jnp.add)
        acc = alpha * acc + _dot(p.astype(BF16), v_ref[ks, :])
        m = m_new
    m_ref[...] = m
    l_ref[...] = l
    acc_ref[...] = acc

    @pl.when(ki == pl.num_programs(3) - 1)
    def _():
        o_ref[...] = (acc * (1.0 / jnp.sum(l, axis=-1, keepdims=True))).astype(o_ref.dtype)


def _mla_attention(q, k, v, b, s, *, tk=8192):
    tq, tk = MLA_TQ, min(tk, s)
    nq, nk = s // tq, s // tk
    stat = pltpu.VMEM((tq, LANES), F32)
    return pl.pallas_call(
        _mla_flash_kernel,
        grid=(b, MLA_HEADS, nq, nk),
        in_specs=[
            pl.BlockSpec((tq, MLA_QPAD), lambda bi, h, qi, ki: (bi * nq + qi, h)),
            pl.BlockSpec((tk, MLA_QPAD), lambda bi, h, qi, ki: (bi * nk + ki, h)),
            pl.BlockSpec((tk, MLA_V), lambda bi, h, qi, ki: (bi * nk + ki, h)),
        ],
        out_specs=pl.BlockSpec((tq, MLA_V), lambda bi, h, qi, ki: (bi * nq + qi, h)),
        out_shape=jax.ShapeDtypeStruct((b * s, MLA_OUT), BF16),
        scratch_shapes=[stat, stat, pltpu.VMEM((tq, MLA_V), F32)],
        compiler_params=_params("parallel", "parallel", "parallel", "arbitrary"),
        name="mla_flash",
    )(q, k, v)


DIL_TQ = 256
DIL_KBLOCKS = 4


def _alibi_slopes(n):
    return 2.0 ** (-8.0 * jnp.arange(1, n + 1, dtype=F32) / n)


def _dil_kernel(slope_ref, q_ref, *refs, n, radius, tq):
    k_refs, v_refs = refs[:DIL_KBLOCKS], refs[DIL_KBLOCKS:2 * DIL_KBLOCKS]
    o_ref, lse_ref = refs[2 * DIL_KBLOCKS:]
    kb = tq // 2
    tk = DIL_KBLOCKS * kb
    i = pl.program_id(2)
    row = lax.broadcasted_iota(jnp.int32, (tq, tk), 0)
    col = lax.broadcasted_iota(jnp.int32, (tq, tk), 1)
    kpos = (2 * i - 1) * kb + col
    dist_i = jnp.abs(row + kb - col)
    valid = (kpos >= 0) & (kpos < n) & (dist_i <= radius)
    dist = dist_i.astype(F32)
    scale = HEAD_DIM ** -0.5

    def window(refs, h):
        sl = slice(h * HEAD_DIM, (h + 1) * HEAD_DIM)
        return jnp.concatenate([ref[..., sl].reshape(kb, HEAD_DIM) for ref in refs], axis=0)

    def scores(h):
        q = q_ref[..., h * HEAD_DIM:(h + 1) * HEAD_DIM].reshape(tq, HEAD_DIM)
        return _dot_t(q, window(k_refs, h))

    s_next = scores(0)
    for h in range(DIL_HEADS_PER_GROUP):
        sl = slice(h * HEAD_DIM, (h + 1) * HEAD_DIM)
        s = s_next
        if h + 1 < DIL_HEADS_PER_GROUP:
            s_next = scores(h + 1)
        v = window(v_refs, h)
        s = s * scale - slope_ref[h] * dist
        s = jnp.where(valid, s, NEG_INF)
        m = jnp.max(s, axis=-1, keepdims=True)
        p = jnp.exp(s - m)
        den = jnp.sum(p, axis=-1, keepdims=True)
        o = _dot(p.astype(BF16), v) * (1.0 / den)
        o_ref[..., sl] = o.reshape(o_ref.shape[:-1] + (HEAD_DIM,))
        lse_ref[..., sl] = jnp.broadcast_to(m + jnp.log(den), (tq, HEAD_DIM)).reshape(o_ref.shape[:-1] + (HEAD_DIM,))


def _dil_group_attention(qkv, slopes, group, b, s):
    window, dil = DIL_GROUPS[group]
    radius = (window // 2) // dil
    n = s // dil
    tq = min(DIL_TQ, n)
    kb = tq // 2
    nblk = n // tq
    assert radius <= kb and s % IN_TM == 0 and n % tq == 0
    t = b * s
    rows = IN_TM // dil
    tiles = s // IN_TM
    q_unit = QKV_TILE0 + NA_TILES + 3 * group

    def spec(br, unit, pick):
        if rows >= br:
            sub = rows // br
            block = (None, None, br, DIL_OUT)
            index = lambda bi, r, u: (bi * tiles + u // sub, r, u % sub, 0)
        else:
            span = br // rows
            assert tiles % span == 0
            block = (span, None, rows, DIL_OUT)
            index = lambda bi, r, u: (bi * tiles // span + u, r, 0, 0)
        if unit is None:
            return pl.BlockSpec(block, lambda bi, r, i: index(bi, r, pick(i)))
        return pl.BlockSpec((None,) + block, lambda bi, r, i: (unit,) + index(bi, r, pick(i)))

    def kv_specs(unit):
        return [spec(kb, unit, lambda i, j=j: jnp.clip(2 * i - 1 + j, 0, n // kb - 1)) for j in range(DIL_KBLOCKS)]

    qkv_v = qkv.reshape(BF_TILES, t // IN_TM, dil, rows, IN_TN)
    kern = functools.partial(_dil_kernel, n=n, radius=radius, tq=tq)
    out_spec = spec(tq, None, lambda i: i)
    o, lse = pl.pallas_call(
        kern,
        grid=(b, dil, nblk),
        in_specs=[pl.BlockSpec(memory_space=pltpu.SMEM), spec(tq, q_unit, lambda i: i)]
        + kv_specs(q_unit + 1) + kv_specs(q_unit + 2),
        out_specs=[out_spec, out_spec],
        out_shape=[jax.ShapeDtypeStruct((t // IN_TM, dil, rows, DIL_OUT), F32)] * 2,
        compiler_params=_params("parallel", "parallel", "arbitrary"),
        name=f"dil_attn_g{group}",
    )(slopes, *([qkv_v] * (1 + 2 * DIL_KBLOCKS)))
    return o, lse


MERGE_TM = 512
NCH = DIL_OUT // LANES


def _merge_kernel(ona_ref, omla_ref, od0_ref, od1_ref, od2_ref, ls0_ref, ls1_ref, ls2_ref,
                  gna_ref, gmla_ref, gdil_ref, wna_ref, wmla_ref, wdil_ref, o_ref, od_ref, tok_ref):
    j = pl.program_id(1)
    tm = MERGE_TM

    @pl.when(j == 0)
    def _():
        for a, (ref, dil) in enumerate(((od1_ref, 4), (ls1_ref, 4), (od2_ref, 16), (ls2_ref, 16))):
            for r in range(dil):
                for c in range(NCH):
                    tok_ref[a, c, pl.ds(r, tm // dil, stride=dil), :] = ref[r, :, c * LANES:(c + 1) * LANES]
        for c in range(NCH):
            cs = slice(c * LANES, (c + 1) * LANES)
            od0, ls0 = od0_ref[0, :, cs], ls0_ref[0, :, cs]
            od1, ls1, od2, ls2 = tok_ref[0, c], tok_ref[1, c], tok_ref[2, c], tok_ref[3, c]
            top = jnp.maximum(jnp.maximum(ls0, ls1), ls2)
            w0, w1, w2 = jnp.exp(ls0 - top), jnp.exp(ls1 - top), jnp.exp(ls2 - top)
            inv = 1.0 / (w0 + w1 + w2)
            od = (w0 * inv) * od0 + (w1 * inv) * od1 + (w2 * inv) * od2
            od_ref[:, cs] = od.astype(BF16)

    gate = lambda ref: ref[...].astype(F32)
    merged = gate(gna_ref) * _dot(ona_ref[...], wna_ref[j])
    merged += gate(gmla_ref) * _dot(omla_ref[...], wmla_ref[j])
    merged += gate(gdil_ref) * _dot(od_ref[...], wdil_ref[j])
    o_ref[...] = merged.astype(o_ref.dtype)


def _col_tiles(w, tn):
    depth, k, n = w.shape
    return w.reshape(depth, k, n // tn, tn).transpose(0, 2, 1, 3)


def _resident(w, layer):
    return pl.BlockSpec((None,) + w.shape[1:], lambda i, j: (layer, 0, 0, 0))


def _merge(o_na, o_mla, o_dil, lse_dil, gates, w_na, w_mla, w_dil, layer):
    t = o_na.shape[0]
    tm = MERGE_TM
    tn = w_na.shape[-1]
    halves = IN_TM // tm
    gstep = D_MODEL // tn
    row = lambda width: pl.BlockSpec((tm, width), lambda i, j: (i, 0))
    gate = lambda g: pl.BlockSpec((None, tm, tn), lambda i, j: (g * gstep + j, i, 0))

    def dil_spec(g):
        dil = DIL_GROUPS[g][1]
        return pl.BlockSpec((None, dil, tm // dil, DIL_OUT), lambda i, j: (i // halves, 0, i % halves, 0))

    return pl.pallas_call(
        _merge_kernel,
        grid=(t // tm, D_MODEL // tn),
        in_specs=[row(NA_WIDTH), row(MLA_OUT)] + [dil_spec(g) for g in range(3)] * 2
        + [gate(0), gate(1), gate(2)] + [_resident(w, layer) for w in (w_na, w_mla, w_dil)],
        out_specs=pl.BlockSpec((tm, tn), lambda i, j: (i, j)),
        out_shape=jax.ShapeDtypeStruct((t, D_MODEL), BF16),
        scratch_shapes=[pltpu.VMEM((tm, DIL_OUT), BF16), pltpu.VMEM((4, NCH, tm, LANES), F32)],
        compiler_params=_params("parallel", "arbitrary"),
        name="merge",
    )(o_na, o_mla, *o_dil, *lse_dil, gates, gates, gates, w_na, w_mla, w_dil)


def _resid_matmul_kernel(a_ref, w_ref, x_ref, o_ref):
    o_ref[...] = x_ref[...] + _dot(a_ref[...], w_ref[pl.program_id(1)])


def _resid_matmul(a, w, x, layer, *, tm=1024):
    t, k = a.shape
    tn = w.shape[-1]
    n = w.shape[1] * tn
    return pl.pallas_call(
        _resid_matmul_kernel,
        grid=(t // tm, n // tn),
        in_specs=[
            pl.BlockSpec((tm, k), lambda i, j: (i, 0)),
            _resident(w, layer),
            pl.BlockSpec((tm, tn), lambda i, j: (i, j)),
        ],
        out_specs=pl.BlockSpec((tm, tn), lambda i, j: (i, j)),
        out_shape=jax.ShapeDtypeStruct((t, n), F32),
        compiler_params=_params("parallel", "arbitrary"),
        name="wo_resid",
    )(a, w, x)


def _prepare(p):
    depth = p["w_in"].shape[0]
    row = lambda g: g[:, None, :].astype(F32)
    w_in = p["w_in"]
    o = np.cumsum((0, NA_WIDTH, NA_WIDTH, NA_WIDTH, MLA_Q_RANK, MLA_KV_RANK, MLA_ROPE,
                   DIL_WIDTH, DIL_WIDTH, DIL_WIDTH, D_MODEL, D_MODEL, D_MODEL))
    seg = lambda a: w_in[:, :, o[a]:o[a + 1]]
    grp = lambda a, g: seg(a)[:, :, g * DIL_OUT:(g + 1) * DIL_OUT]
    rot = np.concatenate([np.arange(MLA_ROPE // 2, MLA_ROPE), np.arange(MLA_ROPE // 2)])
    zpad = jnp.zeros((depth, D_MODEL, LANES - MLA_ROPE), w_in.dtype)
    cols = [seg(9), seg(10), seg(11), seg(0), seg(1), seg(2)]
    for g in range(len(DIL_GROUPS)):
        cols += [grp(6, g), grp(7, g), grp(8, g)]
    cols += [seg(3), seg(4), seg(5), zpad, seg(5)[:, :, rot], zpad]
    w_all = jnp.concatenate(cols, axis=-1).astype(BF16)

    tile = lambda g, reps: jnp.tile(g, (1, reps))
    ones = lambda n: jnp.ones((depth, n), F32)
    dil_gain = jnp.concatenate([tile(p["dil_q_norm"], DIL_HEADS_PER_GROUP), tile(p["dil_k_norm"], DIL_HEADS_PER_GROUP),
                                ones(DIL_OUT)], axis=-1)
    head_gain = jnp.concatenate(
        [tile(p["na_q_norm"], NA_HEADS), tile(p["na_k_norm"], NA_HEADS), ones(NA_WIDTH)]
        + [dil_gain] * len(DIL_GROUPS), axis=-1)

    wuq = p["mla_w_uq"].reshape(depth, MLA_Q_RANK, MLA_HEADS, MLA_QK)
    zq = jnp.zeros((depth, MLA_Q_RANK, MLA_HEADS, LANES - MLA_ROPE), wuq.dtype)
    wuq_rope = wuq[..., MLA_NOPE:]
    wuq = jnp.concatenate([wuq[..., :MLA_NOPE], wuq_rope, zq, wuq_rope[..., rot], zq], axis=-1)
    pad_gain = lambda g: jnp.concatenate([g, jnp.zeros((depth, MLA_QPAD - MLA_QK), F32)], axis=-1)

    return dict(
        ffn1=(row(p["ffn1_norm"]), p["ffn1_w_gate"].astype(BF16), p["ffn1_w_up"].astype(BF16),
              p["ffn1_w_down"].astype(BF16)),
        ffn2=(row(p["ffn2_norm"]), p["ffn2_w_gate"].astype(BF16), p["ffn2_w_up"].astype(BF16),
              p["ffn2_w_down"].astype(BF16)),
        mix_norm=row(p["mix_norm"]),
        w_all=w_all, head_gain=row(head_gain),
        na_bias=[_na_bias_tables(p["na_rpb"][l]) for l in range(depth)],
        cq_gain=row(p["mla_cq_norm"]), ckv_gain=row(p["mla_ckv_norm"]),
        wuq=wuq.reshape(depth, MLA_Q_RANK, MLA_HEADS * MLA_QEXT).astype(BF16),
        wukv=p["mla_w_ukv"].astype(BF16),
        q_gain=row(pad_gain(p["mla_q_norm"])), k_gain=row(pad_gain(p["mla_k_norm"])),
        w_na=_col_tiles(p["w_na_out"].astype(BF16), IN_TN), w_mla=_col_tiles(p["w_mla_out"].astype(BF16), IN_TN),
        w_dil=_col_tiles(p["w_dil_out"].astype(BF16), IN_TN), w_o=_col_tiles(p["w_o"].astype(BF16), IN_TN),
    )


def _dil_slopes():
    slopes = _alibi_slopes(DIL_HEADS)
    out = []
    for g, (_, dil) in enumerate(DIL_GROUPS):
        lo, hi = g * DIL_HEADS_PER_GROUP, (g + 1) * DIL_HEADS_PER_GROUP
        out.append(slopes[lo:hi] * dil)
    return out


def _layer(x, w, layer, b, s, rope, slopes):
    x = _ffn(x, *w["ffn1"], layer)
    qkv, aux = _inproj(x, w["mix_norm"], w["w_all"], w["head_gain"], layer)
    o_na = _na_attention(qkv, w["na_bias"][layer], b, s)
    q, k, v = _mla_prep(aux, rope[0], rope[1], w["cq_gain"], w["ckv_gain"], w["wuq"], w["wukv"],
                        w["q_gain"], w["k_gain"], layer, s)
    o_mla = _mla_attention(q, k, v, b, s)
    dil = [_dil_group_attention(qkv, slopes[g], g, b, s) for g in range(len(DIL_GROUPS))]
    merged = _merge(o_na, o_mla, [d[0] for d in dil], [d[1] for d in dil], qkv,
                    w["w_na"], w["w_mla"], w["w_dil"], layer)
    x = _resid_matmul(merged, w["w_o"], x, layer)
    return _ffn(x, *w["ffn2"], layer)


def _trunk(x, w, slopes):
    b, s, d = x.shape
    rope = _rope_tables(s)
    y = x.reshape(b * s, d)
    for layer in range(w["w_all"].shape[0]):
        y = _layer(y, w, layer, b, s, rope, slopes)
    return y.reshape(b, s, d)


def kernel(x_prompt, x_sample, ffn1_norm, ffn1_w_gate, ffn1_w_up, ffn1_w_down, mix_norm, w_in, na_q_norm, na_k_norm, na_rpb, mla_cq_norm, mla_ckv_norm, mla_w_uq, mla_w_ukv, mla_q_norm, mla_k_norm, dil_q_norm, dil_k_norm, w_na_out, w_mla_out, w_dil_out, w_o, ffn2_norm, ffn2_w_gate, ffn2_w_up, ffn2_w_down):
    w = _prepare(dict(
        ffn1_norm=ffn1_norm, ffn1_w_gate=ffn1_w_gate, ffn1_w_up=ffn1_w_up, ffn1_w_down=ffn1_w_down,
        mix_norm=mix_norm, w_in=w_in, na_q_norm=na_q_norm, na_k_norm=na_k_norm, na_rpb=na_rpb,
        mla_cq_norm=mla_cq_norm, mla_ckv_norm=mla_ckv_norm, mla_w_uq=mla_w_uq, mla_w_ukv=mla_w_ukv,
        mla_q_norm=mla_q_norm, mla_k_norm=mla_k_norm, dil_q_norm=dil_q_norm, dil_k_norm=dil_k_norm,
        w_na_out=w_na_out, w_mla_out=w_mla_out, w_dil_out=w_dil_out, w_o=w_o,
        ffn2_norm=ffn2_norm, ffn2_w_gate=ffn2_w_gate, ffn2_w_up=ffn2_w_up, ffn2_w_down=ffn2_w_down))
    slopes = _dil_slopes()
    return (_trunk(x_prompt, w, slopes), _trunk(x_sample, w, slopes))
```

```python
import functools

import numpy as np
import jax
import jax.numpy as jnp
from jax import lax
from jax.experimental import pallas as pl
from jax.experimental.pallas import tpu as pltpu

F32 = jnp.float32
BF16 = jnp.bfloat16

D_MODEL = 2048
GRID_W = 64
HEAD_DIM = 128
EPS = 1e-6
NEG_INF = -1e30
LOG2E = 1.4426950408889634

NA_HEADS = 8
NA_WIN_R = 8
NA_WIN_C = 16
NA_WIDTH = NA_HEADS * HEAD_DIM

MLA_HEADS = 8
MLA_Q_RANK = 512
MLA_KV_RANK = 256
MLA_NOPE = 128
MLA_ROPE = 64
MLA_V = 128
MLA_QK = MLA_NOPE + MLA_ROPE
MLA_OUT = MLA_HEADS * MLA_V
ROPE_THETA = 10000.0

DIL_GROUPS = ((128, 1), (512, 4), (2048, 16))
DIL_HEADS_PER_GROUP = 4
DIL_HEADS = 12
DIL_WIDTH = DIL_HEADS * HEAD_DIM
DIL_OUT = DIL_HEADS_PER_GROUP * HEAD_DIM

D_FF = 5632
FFN_TF = 512

LANES = 128
VMEM_LIMIT = 56 * 1024 * 1024

IN_TM = 1024
IN_TN = 512
GATE_TILES = 3 * D_MODEL // IN_TN
QKV_TILE0 = GATE_TILES
QKV_TILES = (3 * NA_WIDTH + 3 * DIL_WIDTH) // IN_TN
NA_TILES = 3 * NA_WIDTH // IN_TN
BF_TILES = GATE_TILES + QKV_TILES
BF_COLS = BF_TILES * IN_TN
LAT_COLS = MLA_Q_RANK + MLA_KV_RANK + 2 * LANES
LAT_TILES = LAT_COLS // IN_TN
MLA_QPAD = 2 * LANES
MLA_QEXT = 3 * LANES


def _params(*sem):
    return pltpu.CompilerParams(dimension_semantics=sem, vmem_limit_bytes=VMEM_LIMIT)


def _rms(x, gain):
    return x * lax.rsqrt(jnp.mean(x * x, axis=-1, keepdims=True) + EPS) * gain


def _dot(a, b):
    return jnp.dot(a, b, preferred_element_type=F32)


def _dot_t(a, b):
    return lax.dot_general(a, b, (((1,), (1,)), ((), ())), preferred_element_type=F32)


def _ffn_kernel(x_ref, g_ref, wg_ref, wu_ref, wd_ref, o_ref, xn_ref):
    f = pl.program_id(1)

    @pl.when(f == 0)
    def _():
        x = x_ref[...]
        xn_ref[...] = _rms(x, g_ref[...]).astype(BF16)
        o_ref[...] = x

    xn = xn_ref[...]
    g = _dot(xn, wg_ref[...])
    u = _dot(xn, wu_ref[...])
    h = (g * jax.nn.sigmoid(g) * u * 0.5).astype(BF16)
    o_ref[...] += _dot(h, wd_ref[...])


def _ffn(x, gain, wg, wu, wd, layer, *, tm=1024, tf=FFN_TF):
    t, d = x.shape
    ff = wg.shape[-1]
    return pl.pallas_call(
        _ffn_kernel,
        grid=(t // tm, ff // tf),
        in_specs=[
            pl.BlockSpec((tm, d), lambda i, f: (i, 0)),
            pl.BlockSpec((None, 1, d), lambda i, f: (layer, 0, 0)),
            pl.BlockSpec((None, d, tf), lambda i, f: (layer, 0, f)),
            pl.BlockSpec((None, d, tf), lambda i, f: (layer, 0, f)),
            pl.BlockSpec((None, tf, d), lambda i, f: (layer, f, 0)),
        ],
        out_specs=pl.BlockSpec((tm, d), lambda i, f: (i, 0)),
        out_shape=jax.ShapeDtypeStruct((t, d), F32),
        scratch_shapes=[pltpu.VMEM((tm, d), BF16)],
        compiler_params=_params("parallel", "arbitrary"),
        name="ffn",
    )(x, gain, wg, wu, wd)


def _inproj_kernel(x_ref, g_ref, w_ref, hg_ref, qkv_ref, aux_ref, yn_ref, xn_ref):
    j = pl.program_id(1)
    tm, d = x_ref.shape

    @pl.when(j == 0)
    def _():
        x = x_ref[...]
        inv = lax.rsqrt(jnp.mean(x * x, axis=-1, keepdims=True) + EPS)
        for c in range(d // LANES):
            cs = slice(c * LANES, (c + 1) * LANES)
            y = x_ref[:, cs] * inv * g_ref[:, cs]
            yn_ref[c] = y
            xn_ref[0, :, cs] = y.astype(BF16)
        for g in (1, 2):
            dil = DIL_GROUPS[g][1]
            rows = tm // dil
            for r in range(dil):
                for c in range(d // LANES):
                    xn_ref[g, r * rows:(r + 1) * rows, c * LANES:(c + 1) * LANES] = (
                        yn_ref[c, pl.ds(r, rows, stride=dil), :].astype(BF16))

    jq = j - QKV_TILE0
    is_gate = j < QKV_TILE0
    is_qkv = (jq >= 0) & (jq < QKV_TILES)
    normed = is_qkv & ((jq < 4) | ((jq >= NA_TILES) & ((jq - NA_TILES) % 3 != 2)))
    order = jnp.where(jq < NA_TILES + 3, 0, jnp.where(jq < NA_TILES + 6, 1, 2))

    @pl.when(is_gate)
    def _():
        qkv_ref[...] = jax.nn.sigmoid(_dot(xn_ref[0], w_ref[...])).astype(BF16)

    @pl.when(normed)
    def _():
        acc = _dot(xn_ref[order], w_ref[...])
        hg = hg_ref[...]
        for c in range(acc.shape[1] // HEAD_DIM):
            sl = slice(c * HEAD_DIM, (c + 1) * HEAD_DIM)
            qkv_ref[:, sl] = _rms(acc[:, sl], hg[:, sl]).astype(BF16)

    @pl.when(is_qkv & jnp.logical_not(normed))
    def _():
        qkv_ref[...] = _dot(xn_ref[order], w_ref[...]).astype(BF16)

    @pl.when(j >= BF_TILES)
    def _():
        aux_ref[...] = _dot(xn_ref[0], w_ref[...])


def _inproj(x, gain, w, head_gain, layer):
    t, d = x.shape
    tm, tn = IN_TM, IN_TN
    return pl.pallas_call(
        _inproj_kernel,
        grid=(t // tm, BF_TILES + LAT_TILES),
        in_specs=[
            pl.BlockSpec((tm, d), lambda i, j: (i, 0)),
            pl.BlockSpec((None, 1, d), lambda i, j: (layer, 0, 0)),
            pl.BlockSpec((None, d, tn), lambda i, j: (layer, 0, j)),
            pl.BlockSpec((None, 1, tn), lambda i, j: (layer, 0, jnp.clip(j - QKV_TILE0, 0, QKV_TILES - 1))),
        ],
        out_specs=[
            pl.BlockSpec((None, tm, tn), lambda i, j: (jnp.minimum(j, BF_TILES - 1), i, 0)),
            pl.BlockSpec((tm, tn), lambda i, j: (i, jnp.maximum(j - BF_TILES, 0))),
        ],
        out_shape=[jax.ShapeDtypeStruct((BF_TILES, t, tn), BF16), jax.ShapeDtypeStruct((t, LAT_COLS), F32)],
        scratch_shapes=[pltpu.VMEM((d // LANES, tm, LANES), F32), pltpu.VMEM((3, tm, d), BF16)],
        compiler_params=_params("parallel", "arbitrary"),
        name="inproj",
    )(x, gain, w, head_gain)


NA_QROWS = 4
NA_TQ = NA_QROWS * GRID_W
NA_KROWS = 3 * NA_QROWS
NA_TK = NA_KROWS * GRID_W


def _na_bias_tables(rpb):
    heads = rpb.shape[0]
    qc = np.arange(GRID_W)[:, None]
    kc = np.arange(GRID_W)[None, :]
    col_start = np.clip(qc - NA_WIN_C // 2, 0, GRID_W - NA_WIN_C)
    col_ok = (kc >= col_start) & (kc < col_start + NA_WIN_C)
    dc_idx = np.clip(kc - qc, -(NA_WIN_C - 1), NA_WIN_C - 1) + NA_WIN_C - 1
    onehot = (dc_idx[None] == np.arange(2 * NA_WIN_C - 1)[:, None, None]).astype(np.float32)
    by_col = jnp.einsum("hrd,dqk->hqrk", rpb, jnp.asarray(onehot), precision=lax.Precision.HIGHEST)
    by_col = jnp.where(jnp.asarray(col_ok)[None, :, None, :], by_col, NEG_INF)
    tables = []
    for q_rel, k0_rel in ((0, None), (NA_QROWS, 0), (2 * NA_QROWS, None)):
        rows = []
        for j in range(NA_QROWS):
            k0 = j if k0_rel == 0 else (0 if q_rel == 0 else NA_QROWS)
            dr0 = k0 - (q_rel + j) + NA_WIN_R - 1
            win = by_col[:, :, dr0:dr0 + NA_WIN_R, :]
            pad = lambda n: jnp.full((heads, GRID_W, n, GRID_W), NEG_INF, F32)
            rows.append(jnp.concatenate([pad(k0), win, pad(NA_KROWS - NA_WIN_R - k0)], axis=2))
        tables.append(jnp.stack(rows, axis=1).reshape(heads, NA_TQ, NA_TK))
    return jnp.stack(tables)


def _head_cols(h):
    per_tile = IN_TN // HEAD_DIM
    return h // per_tile, slice((h % per_tile) * HEAD_DIM, (h % per_tile + 1) * HEAD_DIM)


def _na_kernel(q_ref, k0_ref, k1_ref, k2_ref, v0_ref, v1_ref, v2_ref, bias_ref, o_ref):
    scale = HEAD_DIM ** -0.5

    def window(refs, h):
        tile, hs = _head_cols(h)
        return jnp.concatenate([ref[tile, :, hs] for ref in refs], axis=0)

    def scores(h):
        tile, hs = _head_cols(h)
        return _dot_t(q_ref[tile, :, hs], window((k0_ref, k1_ref, k2_ref), h))

    s_next = scores(0)
    for h in range(NA_HEADS):
        s = s_next
        if h + 1 < NA_HEADS:
            s_next = scores(h + 1)
        s = s * scale + bias_ref[h]
        m = jnp.max(s, axis=-1, keepdims=True)
        e = jnp.exp(s - m)
        den = jnp.sum(e, axis=-1, keepdims=True)
        o = _dot(e.astype(BF16), window((v0_ref, v1_ref, v2_ref), h)) * (1.0 / den)
        o_ref[:, h * HEAD_DIM:(h + 1) * HEAD_DIM] = o.astype(o_ref.dtype)


def _na_attention(qkv, bias, b, s):
    nb = s // NA_TQ
    assert nb >= 3
    width = NA_WIDTH // IN_TN
    q_col = QKV_TILE0 // width
    k_col, v_col = q_col + 1, q_col + 2

    def win(i):
        return jnp.clip(i - 1, 0, nb - 3)

    def kv_spec(col, t):
        return pl.BlockSpec((width, NA_TQ, IN_TN), lambda bi, i: (col, bi * nb + win(i) + t, 0))

    def bias_map(bi, i):
        kind = jnp.where(i == 0, 0, jnp.where(i == nb - 1, 2, 1))
        return (kind, 0, 0, 0)

    return pl.pallas_call(
        _na_kernel,
        grid=(b, nb),
        in_specs=[pl.BlockSpec((width, NA_TQ, IN_TN), lambda bi, i: (q_col, bi * nb + i, 0))]
        + [kv_spec(k_col, t) for t in range(3)]
        + [kv_spec(v_col, t) for t in range(3)]
        + [pl.BlockSpec((None, NA_HEADS, NA_TQ, NA_TK), bias_map)],
        out_specs=pl.BlockSpec((NA_TQ, NA_WIDTH), lambda bi, i: (bi * nb + i, 0)),
        out_shape=jax.ShapeDtypeStruct((b * s, NA_WIDTH), BF16),
        compiler_params=_params("parallel", "arbitrary"),
        name="na_attn",
    )(qkv, qkv, qkv, qkv, qkv, qkv, qkv, bias)


def _rope_tables(s):
    half = MLA_ROPE // 2
    inv = ROPE_THETA ** (-jnp.arange(half, dtype=F32) / half)
    ang = jnp.arange(s).astype(F32)[:, None] * inv[None, :]
    cos, sin = jnp.cos(ang), jnp.sin(ang)
    zero = jnp.zeros((s, LANES - MLA_ROPE), F32)
    return (jnp.concatenate([cos, cos, zero], axis=-1), jnp.concatenate([-sin, sin, zero], axis=-1))


def _mla_prep_kernel(cq_ref, ckv_ref, kr_ref, krr_ref, cos_ref, sin_ref, cqg_ref, ckvg_ref, wuq_ref, wukv_ref,
                     qg_ref, kg_ref, q_out, k_out, v_out):
    cos = cos_ref[...]
    sin = sin_ref[...]
    qf = _dot(_rms(cq_ref[...], cqg_ref[...]).astype(BF16), wuq_ref[...])
    kvf = _dot(_rms(ckv_ref[...], ckvg_ref[...]).astype(BF16), wukv_ref[...])
    kr = kr_ref[...] * cos + krr_ref[...] * sin
    kr_ss = jnp.sum(kr * kr, axis=-1, keepdims=True)
    qg = qg_ref[...]
    kg = kg_ref[...]
    for h in range(MLA_HEADS):
        qb = h * MLA_QEXT
        qn = qf[:, qb:qb + LANES]
        qr = qf[:, qb + LANES:qb + 2 * LANES] * cos + qf[:, qb + 2 * LANES:qb + 3 * LANES] * sin
        ms = (jnp.sum(qn * qn, axis=-1, keepdims=True) + jnp.sum(qr * qr, axis=-1, keepdims=True)) * (1.0 / MLA_QK)
        r = lax.rsqrt(ms + EPS)
        ob = h * MLA_QPAD
        q_out[:, ob:ob + LANES] = (qn * r * qg[:, :LANES]).astype(BF16)
        q_out[:, ob + LANES:ob + 2 * LANES] = (qr * r * qg[:, LANES:]).astype(BF16)
        kb = h * (MLA_NOPE + MLA_V)
        kn = kvf[:, kb:kb + MLA_NOPE]
        msk = (jnp.sum(kn * kn, axis=-1, keepdims=True) + kr_ss) * (1.0 / MLA_QK)
        rk = lax.rsqrt(msk + EPS)
        k_out[:, ob:ob + LANES] = (kn * rk * kg[:, :LANES]).astype(BF16)
        k_out[:, ob + LANES:ob + 2 * LANES] = (kr * rk * kg[:, LANES:]).astype(BF16)
        v_out[:, h * MLA_V:(h + 1) * MLA_V] = kvf[:, kb + MLA_NOPE:kb + MLA_NOPE + MLA_V].astype(BF16)


def _mla_prep(aux, cos, sin, cq_gain, ckv_gain, wuq, wukv, q_gain, k_gain, layer, s, *, tm=256):
    t = aux.shape[0]
    n_pos = s // tm
    full = lambda shape: pl.BlockSpec((None,) + shape, lambda i: (layer,) + (0,) * len(shape))
    return pl.pallas_call(
        _mla_prep_kernel,
        grid=(t // tm,),
        in_specs=[
            pl.BlockSpec((tm, MLA_Q_RANK), lambda i: (i, 0)),
            pl.BlockSpec((tm, MLA_KV_RANK), lambda i: (i, MLA_Q_RANK // MLA_KV_RANK)),
            pl.BlockSpec((tm, LANES), lambda i: (i, (MLA_Q_RANK + MLA_KV_RANK) // LANES)),
            pl.BlockSpec((tm, LANES), lambda i: (i, (MLA_Q_RANK + MLA_KV_RANK) // LANES + 1)),
            pl.BlockSpec((tm, LANES), lambda i: (i % n_pos, 0)),
            pl.BlockSpec((tm, LANES), lambda i: (i % n_pos, 0)),
            full((1, MLA_Q_RANK)),
            full((1, MLA_KV_RANK)),
            full((MLA_Q_RANK, MLA_HEADS * MLA_QEXT)),
            full((MLA_KV_RANK, MLA_HEADS * (MLA_NOPE + MLA_V))),
            full((1, MLA_QPAD)),
            full((1, MLA_QPAD)),
        ],
        out_specs=[
            pl.BlockSpec((tm, MLA_HEADS * MLA_QPAD), lambda i: (i, 0)),
            pl.BlockSpec((tm, MLA_HEADS * MLA_QPAD), lambda i: (i, 0)),
            pl.BlockSpec((tm, MLA_OUT), lambda i: (i, 0)),
        ],
        out_shape=[
            jax.ShapeDtypeStruct((t, MLA_HEADS * MLA_QPAD), BF16),
            jax.ShapeDtypeStruct((t, MLA_HEADS * MLA_QPAD), BF16),
            jax.ShapeDtypeStruct((t, MLA_OUT), BF16),
        ],
        compiler_params=_params("parallel"),
        name="mla_prep",
    )(aux, aux, aux, aux, cos, sin, cq_gain, ckv_gain, wuq, wukv, q_gain, k_gain)


MLA_TQ = 1024
MLA_KCHUNK = 512


def _lane_groups(x, op):
    out = x[:, :LANES]
    for c in range(1, x.shape[1] // LANES):
        out = op(out, x[:, c * LANES:(c + 1) * LANES])
    return out


def _mla_flash_kernel(q_ref, k_ref, v_ref, o_ref, m_ref, l_ref, acc_ref):
    ki = pl.program_id(3)

    @pl.when(ki == 0)
    def _():
        m_ref[...] = jnp.full_like(m_ref, -jnp.inf)
        l_ref[...] = jnp.zeros_like(l_ref)
        acc_ref[...] = jnp.zeros_like(acc_ref)

    q = q_ref[...]
    c2 = (MLA_QK ** -0.5) * LOG2E
    m, l, acc = m_ref[...], l_ref[...], acc_ref[...]
    for c in range(k_ref.shape[0] // MLA_KCHUNK):
        ks = slice(c * MLA_KCHUNK, (c + 1) * MLA_KCHUNK)
        s = _dot_t(q, k_ref[ks, :]) * c2
        m_new = jnp.maximum(m, jnp.max(_lane_groups(s, jnp.maximum), axis=-1, keepdims=True))
        alpha = jnp.exp2(m - m_new)
        p = jnp.exp2(s - jnp.concatenate([m_new] * (MLA_KCHUNK // LANES), axis=1))
        l = alpha * l + _lane_groups(p, jnp.add)
        acc = alpha * acc + _dot(p.astype(BF16), v_ref[ks, :])
        m = m_new
    m_ref[...] = m
    l_ref[...] = l
    acc_ref[...] = acc

    @pl.when(ki == pl.num_programs(3) - 1)
    def _():
        o_ref[...] = (acc * (1.0 / jnp.sum(l, axis=-1, keepdims=True))).astype(o_ref.dtype)


def _mla_attention(q, k, v, b, s, *, tk=8192):
    tq, tk = MLA_TQ, min(tk, s)
    nq, nk = s // tq, s // tk
    stat = pltpu.VMEM((tq, LANES), F32)
    return pl.pallas_call(
        _mla_flash_kernel,
        grid=(b, MLA_HEADS, nq, nk),
        in_specs=[
            pl.BlockSpec((tq, MLA_QPAD), lambda bi, h, qi, ki: (bi * nq + qi, h)),
            pl.BlockSpec((tk, MLA_QPAD), lambda bi, h, qi, ki: (bi * nk + ki, h)),
            pl.BlockSpec((tk, MLA_V), lambda bi, h, qi, ki: (bi * nk + ki, h)),
        ],
        out_specs=pl.BlockSpec((tq, MLA_V), lambda bi, h, qi, ki: (bi * nq + qi, h)),
        out_shape=jax.ShapeDtypeStruct((b * s, MLA_OUT), BF16),
        scratch_shapes=[stat, stat, pltpu.VMEM((tq, MLA_V), F32)],
        compiler_params=_params("parallel", "parallel", "parallel", "arbitrary"),
        name="mla_flash",
    )(q, k, v)


DIL_TQ = 256
DIL_KBLOCKS = 4


def _alibi_slopes(n):
    return 2.0 ** (-8.0 * jnp.arange(1, n + 1, dtype=F32) / n)


def _dil_kernel(slope_ref, q_ref, *refs, n, radius, tq):
    k_refs, v_refs = refs[:DIL_KBLOCKS], refs[DIL_KBLOCKS:2 * DIL_KBLOCKS]
    o_ref, lse_ref = refs[2 * DIL_KBLOCKS:]
    kb = tq // 2
    tk = DIL_KBLOCKS * kb
    i = pl.program_id(2)
    row = lax.broadcasted_iota(jnp.int32, (tq, tk), 0)
    col = lax.broadcasted_iota(jnp.int32, (tq, tk), 1)
    kpos = (2 * i - 1) * kb + col
    dist_i = jnp.abs(row + kb - col)
    valid = (kpos >= 0) & (kpos < n) & (dist_i <= radius)
    dist = dist_i.astype(F32)
    scale = HEAD_DIM ** -0.5

    def window(refs, h):
        sl = slice(h * HEAD_DIM, (h + 1) * HEAD_DIM)
        return jnp.concatenate([ref[..., sl].reshape(kb, HEAD_DIM) for ref in refs], axis=0)

    def scores(h):
        q = q_ref[..., h * HEAD_DIM:(h + 1) * HEAD_DIM].reshape(tq, HEAD_DIM)
        return _dot_t(q, window(k_refs, h))

    s_next = scores(0)
    for h in range(DIL_HEADS_PER_GROUP):
        sl = slice(h * HEAD_DIM, (h + 1) * HEAD_DIM)
        s = s_next
        if h + 1 < DIL_HEADS_PER_GROUP:
            s_next = scores(h + 1)
        v = window(v_refs, h)
        s = s * scale - slope_ref[h] * dist
        s = jnp.where(valid, s, NEG_INF)
        m = jnp.max(s, axis=-1, keepdims=True)
        p = jnp.exp(s - m)
        den = jnp.sum(p, axis=-1, keepdims=True)
        o = _dot(p.astype(BF16), v) * (1.0 / den)
        o_ref[..., sl] = o.reshape(o_ref.shape[:-1] + (HEAD_DIM,))
        lse_ref[..., sl] = jnp.broadcast_to(m + jnp.log(den), (tq, HEAD_DIM)).reshape(o_ref.shape[:-1] + (HEAD_DIM,))


def _dil_group_attention(qkv, slopes, group, b, s):
    window, dil = DIL_GROUPS[group]
    radius = (window // 2) // dil
    n = s // dil
    tq = min(DIL_TQ, n)
    kb = tq // 2
    nblk = n // tq
    assert radius <= kb and s % IN_TM == 0 and n % tq == 0
    t = b * s
    rows = IN_TM // dil
    tiles = s // IN_TM
    q_unit = QKV_TILE0 + NA_TILES + 3 * group

    def spec(br, unit, pick):
        if rows >= br:
            sub = rows // br
            block = (None, None, br, DIL_OUT)
            index = lambda bi, r, u: (bi * tiles + u // sub, r, u % sub, 0)
        else:
            span = br // rows
            assert tiles % span == 0
            block = (span, None, rows, DIL_OUT)
            index = lambda bi, r, u: (bi * tiles // span + u, r, 0, 0)
        if unit is None:
            return pl.BlockSpec(block, lambda bi, r, i: index(bi, r, pick(i)))
        return pl.BlockSpec((None,) + block, lambda bi, r, i: (unit,) + index(bi, r, pick(i)))

    def kv_specs(unit):
        return [spec(kb, unit, lambda i, j=j: jnp.clip(2 * i - 1 + j, 0, n // kb - 1)) for j in range(DIL_KBLOCKS)]

    qkv_v = qkv.reshape(BF_TILES, t // IN_TM, dil, rows, IN_TN)
    kern = functools.partial(_dil_kernel, n=n, radius=radius, tq=tq)
    out_spec = spec(tq, None, lambda i: i)
    o, lse = pl.pallas_call(
        kern,
        grid=(b, dil, nblk),
        in_specs=[pl.BlockSpec(memory_space=pltpu.SMEM), spec(tq, q_unit, lambda i: i)]
        + kv_specs(q_unit + 1) + kv_specs(q_unit + 2),
        out_specs=[out_spec, out_spec],
        out_shape=[jax.ShapeDtypeStruct((t // IN_TM, dil, rows, DIL_OUT), F32)] * 2,
        compiler_params=_params("parallel", "parallel", "arbitrary"),
        name=f"dil_attn_g{group}",
    )(slopes, *([qkv_v] * (1 + 2 * DIL_KBLOCKS)))
    return o, lse


MERGE_TM = 512
NCH = DIL_OUT // LANES


def _merge_kernel(ona_ref, omla_ref, od0_ref, od1_ref, od2_ref, ls0_ref, ls1_ref, ls2_ref,
                  gna_ref, gmla_ref, gdil_ref, wna_ref, wmla_ref, wdil_ref, o_ref, od_ref, tok_ref):
    tm = MERGE_TM
    for a, (ref, dil) in enumerate(((od1_ref, 4), (ls1_ref, 4), (od2_ref, 16), (ls2_ref, 16))):
        for r in range(dil):
            for c in range(NCH):
                tok_ref[a, c, pl.ds(r, tm // dil, stride=dil), :] = ref[r, :, c * LANES:(c + 1) * LANES]
    for c in range(NCH):
        cs = slice(c * LANES, (c + 1) * LANES)
        od0, ls0 = od0_ref[0, :, cs], ls0_ref[0, :, cs]
        od1, ls1, od2, ls2 = tok_ref[0, c], tok_ref[1, c], tok_ref[2, c], tok_ref[3, c]
        top = jnp.maximum(jnp.maximum(ls0, ls1), ls2)
        w0, w1, w2 = jnp.exp(ls0 - top), jnp.exp(ls1 - top), jnp.exp(ls2 - top)
        inv = 1.0 / (w0 + w1 + w2)
        od = (w0 * inv) * od0 + (w1 * inv) * od1 + (w2 * inv) * od2
        od_ref[:, cs] = od.astype(BF16)

    o_na, o_mla, o_dil = ona_ref[...], omla_ref[...], od_ref[...]
    tn = wna_ref.shape[-1]
    for j in range(wna_ref.shape[0]):
        merged = gna_ref[j].astype(F32) * _dot(o_na, wna_ref[j])
        merged += gmla_ref[j].astype(F32) * _dot(o_mla, wmla_ref[j])
        merged += gdil_ref[j].astype(F32) * _dot(o_dil, wdil_ref[j])
        o_ref[:, j * tn:(j + 1) * tn] = merged.astype(o_ref.dtype)


def _col_tiles(w, tn):
    depth, k, n = w.shape
    return w.reshape(depth, k, n // tn, tn).transpose(0, 2, 1, 3)


def _resident(w, layer):
    return pl.BlockSpec((None,) + w.shape[1:], lambda i, j: (layer, 0, 0, 0))


def _merge(o_na, o_mla, o_dil, lse_dil, gates, w_na, w_mla, w_dil, layer):
    t = o_na.shape[0]
    tm = MERGE_TM
    tn = w_na.shape[-1]
    halves = IN_TM // tm
    gstep = D_MODEL // tn
    row = lambda width: pl.BlockSpec((tm, width), lambda i: (i, 0))
    gate = lambda g: pl.BlockSpec((gstep, tm, tn), lambda i: (g, i, 0))
    weight = lambda w: pl.BlockSpec((None,) + w.shape[1:], lambda i: (layer, 0, 0, 0), pipeline_mode=pl.Buffered(1))

    def dil_spec(g):
        dil = DIL_GROUPS[g][1]
        return pl.BlockSpec((None, dil, tm // dil, DIL_OUT), lambda i: (i // halves, 0, i % halves, 0))

    return pl.pallas_call(
        _merge_kernel,
        grid=(t // tm,),
        in_specs=[row(NA_WIDTH), row(MLA_OUT)] + [dil_spec(g) for g in range(3)] * 2
        + [gate(0), gate(1), gate(2)] + [weight(w) for w in (w_na, w_mla, w_dil)],
        out_specs=pl.BlockSpec((tm, D_MODEL), lambda i: (i, 0)),
        out_shape=jax.ShapeDtypeStruct((t, D_MODEL), BF16),
        scratch_shapes=[pltpu.VMEM((tm, DIL_OUT), BF16), pltpu.VMEM((4, NCH, tm, LANES), F32)],
        compiler_params=_params("parallel"),
        name="merge",
    )(o_na, o_mla, *o_dil, *lse_dil, gates, gates, gates, w_na, w_mla, w_dil)


def _resid_matmul_kernel(a_ref, w_ref, x_ref, o_ref):
    o_ref[...] = x_ref[...] + _dot(a_ref[...], w_ref[pl.program_id(1)])


def _resid_matmul(a, w, x, layer, *, tm=1024):
    t, k = a.shape
    tn = w.shape[-1]
    n = w.shape[1] * tn
    return pl.pallas_call(
        _resid_matmul_kernel,
        grid=(t // tm, n // tn),
        in_specs=[
            pl.BlockSpec((tm, k), lambda i, j: (i, 0)),
            _resident(w, layer),
            pl.BlockSpec((tm, tn), lambda i, j: (i, j)),
        ],
        out_specs=pl.BlockSpec((tm, tn), lambda i, j: (i, j)),
        out_shape=jax.ShapeDtypeStruct((t, n), F32),
        compiler_params=_params("parallel", "arbitrary"),
        name="wo_resid",
    )(a, w, x)


def _prepare(p):
    depth = p["w_in"].shape[0]
    row = lambda g: g[:, None, :].astype(F32)
    w_in = p["w_in"]
    o = np.cumsum((0, NA_WIDTH, NA_WIDTH, NA_WIDTH, MLA_Q_RANK, MLA_KV_RANK, MLA_ROPE,
                   DIL_WIDTH, DIL_WIDTH, DIL_WIDTH, D_MODEL, D_MODEL, D_MODEL))
    seg = lambda a: w_in[:, :, o[a]:o[a + 1]]
    grp = lambda a, g: seg(a)[:, :, g * DIL_OUT:(g + 1) * DIL_OUT]
    rot = np.concatenate([np.arange(MLA_ROPE // 2, MLA_ROPE), np.arange(MLA_ROPE // 2)])
    zpad = jnp.zeros((depth, D_MODEL, LANES - MLA_ROPE), w_in.dtype)
    cols = [seg(9), seg(10), seg(11), seg(0), seg(1), seg(2)]
    for g in range(len(DIL_GROUPS)):
        cols += [grp(6, g), grp(7, g), grp(8, g)]
    cols += [seg(3), seg(4), seg(5), zpad, seg(5)[:, :, rot], zpad]
    w_all = jnp.concatenate(cols, axis=-1).astype(BF16)

    tile = lambda g, reps: jnp.tile(g, (1, reps))
    ones = lambda n: jnp.ones((depth, n), F32)
    dil_gain = jnp.concatenate([tile(p["dil_q_norm"], DIL_HEADS_PER_GROUP), tile(p["dil_k_norm"], DIL_HEADS_PER_GROUP),
                                ones(DIL_OUT)], axis=-1)
    head_gain = jnp.concatenate(
        [tile(p["na_q_norm"], NA_HEADS), tile(p["na_k_norm"], NA_HEADS), ones(NA_WIDTH)]
        + [dil_gain] * len(DIL_GROUPS), axis=-1)

    wuq = p["mla_w_uq"].reshape(depth, MLA_Q_RANK, MLA_HEADS, MLA_QK)
    zq = jnp.zeros((depth, MLA_Q_RANK, MLA_HEADS, LANES - MLA_ROPE), wuq.dtype)
    wuq_rope = wuq[..., MLA_NOPE:]
    wuq = jnp.concatenate([wuq[..., :MLA_NOPE], wuq_rope, zq, wuq_rope[..., rot], zq], axis=-1)
    pad_gain = lambda g: jnp.concatenate([g, jnp.zeros((depth, MLA_QPAD - MLA_QK), F32)], axis=-1)

    return dict(
        ffn1=(row(p["ffn1_norm"]), p["ffn1_w_gate"].astype(BF16), p["ffn1_w_up"].astype(BF16),
              p["ffn1_w_down"].astype(BF16)),
        ffn2=(row(p["ffn2_norm"]), p["ffn2_w_gate"].astype(BF16), p["ffn2_w_up"].astype(BF16),
              p["ffn2_w_down"].astype(BF16)),
        mix_norm=row(p["mix_norm"]),
        w_all=w_all, head_gain=row(head_gain),
        na_bias=[_na_bias_tables(p["na_rpb"][l]) for l in range(depth)],
        cq_gain=row(p["mla_cq_norm"]), ckv_gain=row(p["mla_ckv_norm"]),
        wuq=wuq.reshape(depth, MLA_Q_RANK, MLA_HEADS * MLA_QEXT).astype(BF16),
        wukv=p["mla_w_ukv"].astype(BF16),
        q_gain=row(pad_gain(p["mla_q_norm"])), k_gain=row(pad_gain(p["mla_k_norm"])),
        w_na=_col_tiles(p["w_na_out"].astype(BF16), IN_TN), w_mla=_col_tiles(p["w_mla_out"].astype(BF16), IN_TN),
        w_dil=_col_tiles(p["w_dil_out"].astype(BF16), IN_TN), w_o=_col_tiles(p["w_o"].astype(BF16), IN_TN),
    )


def _dil_slopes():
    slopes = _alibi_slopes(DIL_HEADS)
    out = []
    for g, (_, dil) in enumerate(DIL_GROUPS):
        lo, hi = g * DIL_HEADS_PER_GROUP, (g + 1) * DIL_HEADS_PER_GROUP
        out.append(slopes[lo:hi] * dil)
    return out


def _layer(x, w, layer, b, s, rope, slopes):
    x = _ffn(x, *w["ffn1"], layer)
    qkv, aux = _inproj(x, w["mix_norm"], w["w_all"], w["head_gain"], layer)
    o_na = _na_attention(qkv, w["na_bias"][layer], b, s)
    q, k, v = _mla_prep(aux, rope[0], rope[1], w["cq_gain"], w["ckv_gain"], w["wuq"], w["wukv"],
                        w["q_gain"], w["k_gain"], layer, s)
    o_mla = _mla_attention(q, k, v, b, s)
    dil = [_dil_group_attention(qkv, slopes[g], g, b, s) for g in range(len(DIL_GROUPS))]
    merged = _merge(o_na, o_mla, [d[0] for d in dil], [d[1] for d in dil], qkv,
                    w["w_na"], w["w_mla"], w["w_dil"], layer)
    x = _resid_matmul(merged, w["w_o"], x, layer)
    return _ffn(x, *w["ffn2"], layer)


def _trunk(x, w, slopes):
    b, s, d = x.shape
    rope = _rope_tables(s)
    y = x.reshape(b * s, d)
    for layer in range(w["w_all"].shape[0]):
        y = _layer(y, w, layer, b, s, rope, slopes)
    return y.reshape(b, s, d)


def kernel(x_prompt, x_sample, ffn1_norm, ffn1_w_gate, ffn1_w_up, ffn1_w_down, mix_norm, w_in, na_q_norm, na_k_norm, na_rpb, mla_cq_norm, mla_ckv_norm, mla_w_uq, mla_w_ukv, mla_q_norm, mla_k_norm, dil_q_norm, dil_k_norm, w_na_out, w_mla_out, w_dil_out, w_o, ffn2_norm, ffn2_w_gate, ffn2_w_up, ffn2_w_down):
    w = _prepare(dict(
        ffn1_norm=ffn1_norm, ffn1_w_gate=ffn1_w_gate, ffn1_w_up=ffn1_w_up, ffn1_w_down=ffn1_w_down,
        mix_norm=mix_norm, w_in=w_in, na_q_norm=na_q_norm, na_k_norm=na_k_norm, na_rpb=na_rpb,
        mla_cq_norm=mla_cq_norm, mla_ckv_norm=mla_ckv_norm, mla_w_uq=mla_w_uq, mla_w_ukv=mla_w_ukv,
        mla_q_norm=mla_q_norm, mla_k_norm=mla_k_norm, dil_q_norm=dil_q_norm, dil_k_norm=dil_k_norm,
        w_na_out=w_na_out, w_mla_out=w_mla_out, w_dil_out=w_dil_out, w_o=w_o,
        ffn2_norm=ffn2_norm, ffn2_w_gate=ffn2_w_gate, ffn2_w_up=ffn2_w_up, ffn2_w_down=ffn2_w_down))
    slopes = _dil_slopes()
    return (_trunk(x_prompt, w, slopes), _trunk(x_sample, w, slopes))
```

```python
import functools

import numpy as np
import jax
import jax.numpy as jnp
from jax import lax
from jax.experimental import pallas as pl
from jax.experimental.pallas import tpu as pltpu

F32 = jnp.float32
BF16 = jnp.bfloat16

D_MODEL = 2048
GRID_W = 64
HEAD_DIM = 128
EPS = 1e-6
NEG_INF = -1e30
LOG2E = 1.4426950408889634

NA_HEADS = 8
NA_WIN_R = 8
NA_WIN_C = 16
NA_WIDTH = NA_HEADS * HEAD_DIM

MLA_HEADS = 8
MLA_Q_RANK = 512
MLA_KV_RANK = 256
MLA_NOPE = 128
MLA_ROPE = 64
MLA_V = 128
MLA_QK = MLA_NOPE + MLA_ROPE
MLA_OUT = MLA_HEADS * MLA_V
ROPE_THETA = 10000.0

DIL_GROUPS = ((128, 1), (512, 4), (2048, 16))
DIL_HEADS_PER_GROUP = 4
DIL_HEADS = 12
DIL_WIDTH = DIL_HEADS * HEAD_DIM
DIL_OUT = DIL_HEADS_PER_GROUP * HEAD_DIM

D_FF = 5632
FFN_TF = 512

LANES = 128
VMEM_LIMIT = 56 * 1024 * 1024

IN_TM = 1024
IN_TN = 512
GATE_TILES = 3 * D_MODEL // IN_TN
QKV_TILE0 = GATE_TILES
QKV_TILES = (3 * NA_WIDTH + 3 * DIL_WIDTH) // IN_TN
NA_TILES = 3 * NA_WIDTH // IN_TN
BF_TILES = GATE_TILES + QKV_TILES
BF_COLS = BF_TILES * IN_TN
LAT_COLS = MLA_Q_RANK + MLA_KV_RANK + 2 * LANES
LAT_TILES = LAT_COLS // IN_TN
MLA_QPAD = 2 * LANES
MLA_QEXT = 3 * LANES


def _params(*sem):
    return pltpu.CompilerParams(dimension_semantics=sem, vmem_limit_bytes=VMEM_LIMIT)


def _rms(x, gain):
    return x * lax.rsqrt(jnp.mean(x * x, axis=-1, keepdims=True) + EPS) * gain


def _dot(a, b):
    return jnp.dot(a, b, preferred_element_type=F32)


def _dot_t(a, b):
    return lax.dot_general(a, b, (((1,), (1,)), ((), ())), preferred_element_type=F32)


def _ffn_kernel(x_ref, g_ref, wg_ref, wu_ref, wd_ref, o_ref, xn_ref):
    f = pl.program_id(1)

    @pl.when(f == 0)
    def _():
        x = x_ref[...]
        xn_ref[...] = _rms(x, g_ref[...]).astype(BF16)
        o_ref[...] = x

    xn = xn_ref[...]
    g = _dot(xn, wg_ref[...])
    u = _dot(xn, wu_ref[...])
    h = (g * jax.nn.sigmoid(g) * u * 0.5).astype(BF16)
    o_ref[...] += _dot(h, wd_ref[...])


def _ffn(x, gain, wg, wu, wd, layer, *, tm=1024, tf=FFN_TF):
    t, d = x.shape
    ff = wg.shape[-1]
    return pl.pallas_call(
        _ffn_kernel,
        grid=(t // tm, ff // tf),
        in_specs=[
            pl.BlockSpec((tm, d), lambda i, f: (i, 0)),
            pl.BlockSpec((None, 1, d), lambda i, f: (layer, 0, 0)),
            pl.BlockSpec((None, d, tf), lambda i, f: (layer, 0, f)),
            pl.BlockSpec((None, d, tf), lambda i, f: (layer, 0, f)),
            pl.BlockSpec((None, tf, d), lambda i, f: (layer, f, 0)),
        ],
        out_specs=pl.BlockSpec((tm, d), lambda i, f: (i, 0)),
        out_shape=jax.ShapeDtypeStruct((t, d), F32),
        scratch_shapes=[pltpu.VMEM((tm, d), BF16)],
        compiler_params=_params("parallel", "arbitrary"),
        name="ffn",
    )(x, gain, wg, wu, wd)


def _inproj_kernel(x_ref, g_ref, w_ref, hg_ref, qkv_ref, aux_ref, yn_ref, xn_ref):
    j = pl.program_id(1)
    tm, d = x_ref.shape

    @pl.when(j == 0)
    def _():
        x = x_ref[...]
        inv = lax.rsqrt(jnp.mean(x * x, axis=-1, keepdims=True) + EPS)
        for c in range(d // LANES):
            cs = slice(c * LANES, (c + 1) * LANES)
            y = x_ref[:, cs] * inv * g_ref[:, cs]
            yn_ref[c] = y
            xn_ref[0, :, cs] = y.astype(BF16)
        for g in (1, 2):
            dil = DIL_GROUPS[g][1]
            rows = tm // dil
            for r in range(dil):
                for c in range(d // LANES):
                    xn_ref[g, r * rows:(r + 1) * rows, c * LANES:(c + 1) * LANES] = (
                        yn_ref[c, pl.ds(r, rows, stride=dil), :].astype(BF16))

    jq = j - QKV_TILE0
    is_gate = j < QKV_TILE0
    is_qkv = (jq >= 0) & (jq < QKV_TILES)
    normed = is_qkv & ((jq < 4) | ((jq >= NA_TILES) & ((jq - NA_TILES) % 3 != 2)))
    order = jnp.where(jq < NA_TILES + 3, 0, jnp.where(jq < NA_TILES + 6, 1, 2))

    @pl.when(is_gate)
    def _():
        qkv_ref[...] = jax.nn.sigmoid(_dot(xn_ref[0], w_ref[...])).astype(BF16)

    @pl.when(normed)
    def _():
        acc = _dot(xn_ref[order], w_ref[...])
        hg = hg_ref[...]
        for c in range(acc.shape[1] // HEAD_DIM):
            sl = slice(c * HEAD_DIM, (c + 1) * HEAD_DIM)
            qkv_ref[:, sl] = _rms(acc[:, sl], hg[:, sl]).astype(BF16)

    @pl.when(is_qkv & jnp.logical_not(normed))
    def _():
        qkv_ref[...] = _dot(xn_ref[order], w_ref[...]).astype(BF16)

    @pl.when(j >= BF_TILES)
    def _():
        aux_ref[...] = _dot(xn_ref[0], w_ref[...])


def _inproj(x, gain, w, head_gain, layer):
    t, d = x.shape
    tm, tn = IN_TM, IN_TN
    return pl.pallas_call(
        _inproj_kernel,
        grid=(t // tm, BF_TILES + LAT_TILES),
        in_specs=[
            pl.BlockSpec((tm, d), lambda i, j: (i, 0)),
            pl.BlockSpec((None, 1, d), lambda i, j: (layer, 0, 0)),
            pl.BlockSpec((None, d, tn), lambda i, j: (layer, 0, j)),
            pl.BlockSpec((None, 1, tn), lambda i, j: (layer, 0, jnp.clip(j - QKV_TILE0, 0, QKV_TILES - 1))),
        ],
        out_specs=[
            pl.BlockSpec((None, tm, tn), lambda i, j: (jnp.minimum(j, BF_TILES - 1), i, 0)),
            pl.BlockSpec((tm, tn), lambda i, j: (i, jnp.maximum(j - BF_TILES, 0))),
        ],
        out_shape=[jax.ShapeDtypeStruct((BF_TILES, t, tn), BF16), jax.ShapeDtypeStruct((t, LAT_COLS), F32)],
        scratch_shapes=[pltpu.VMEM((d // LANES, tm, LANES), F32), pltpu.VMEM((3, tm, d), BF16)],
        compiler_params=_params("parallel", "arbitrary"),
        name="inproj",
    )(x, gain, w, head_gain)


NA_QROWS = 4
NA_TQ = NA_QROWS * GRID_W
NA_KROWS = 3 * NA_QROWS
NA_TK = NA_KROWS * GRID_W


def _na_bias_tables(rpb):
    heads = rpb.shape[0]
    qc = np.arange(GRID_W)[:, None]
    kc = np.arange(GRID_W)[None, :]
    col_start = np.clip(qc - NA_WIN_C // 2, 0, GRID_W - NA_WIN_C)
    col_ok = (kc >= col_start) & (kc < col_start + NA_WIN_C)
    dc_idx = np.clip(kc - qc, -(NA_WIN_C - 1), NA_WIN_C - 1) + NA_WIN_C - 1
    onehot = (dc_idx[None] == np.arange(2 * NA_WIN_C - 1)[:, None, None]).astype(np.float32)
    by_col = jnp.einsum("hrd,dqk->hqrk", rpb, jnp.asarray(onehot), precision=lax.Precision.HIGHEST)
    by_col = jnp.where(jnp.asarray(col_ok)[None, :, None, :], by_col, NEG_INF)
    tables = []
    for q_rel, k0_rel in ((0, None), (NA_QROWS, 0), (2 * NA_QROWS, None)):
        rows = []
        for j in range(NA_QROWS):
            k0 = j if k0_rel == 0 else (0 if q_rel == 0 else NA_QROWS)
            dr0 = k0 - (q_rel + j) + NA_WIN_R - 1
            win = by_col[:, :, dr0:dr0 + NA_WIN_R, :]
            pad = lambda n: jnp.full((heads, GRID_W, n, GRID_W), NEG_INF, F32)
            rows.append(jnp.concatenate([pad(k0), win, pad(NA_KROWS - NA_WIN_R - k0)], axis=2))
        tables.append(jnp.stack(rows, axis=1).reshape(heads, NA_TQ, NA_TK))
    return jnp.stack(tables)


def _head_cols(h):
    per_tile = IN_TN // HEAD_DIM
    return h // per_tile, slice((h % per_tile) * HEAD_DIM, (h % per_tile + 1) * HEAD_DIM)


def _na_kernel(q_ref, k0_ref, k1_ref, k2_ref, v0_ref, v1_ref, v2_ref, bias_ref, o_ref):
    scale = HEAD_DIM ** -0.5

    def window(refs, h):
        tile, hs = _head_cols(h)
        return jnp.concatenate([ref[tile, :, hs] for ref in refs], axis=0)

    def scores(h):
        tile, hs = _head_cols(h)
        return _dot_t(q_ref[tile, :, hs], window((k0_ref, k1_ref, k2_ref), h))

    s_next = scores(0)
    for h in range(NA_HEADS):
        s = s_next
        if h + 1 < NA_HEADS:
            s_next = scores(h + 1)
        s = s * scale + bias_ref[h]
        m = jnp.max(s, axis=-1, keepdims=True)
        e = jnp.exp(s - m)
        den = jnp.sum(e, axis=-1, keepdims=True)
        o = _dot(e.astype(BF16), window((v0_ref, v1_ref, v2_ref), h)) * (1.0 / den)
        o_ref[:, h * HEAD_DIM:(h + 1) * HEAD_DIM] = o.astype(o_ref.dtype)


def _na_attention(qkv, bias, b, s):
    nb = s // NA_TQ
    assert nb >= 3
    width = NA_WIDTH // IN_TN
    q_col = QKV_TILE0 // width
    k_col, v_col = q_col + 1, q_col + 2

    def win(i):
        return jnp.clip(i - 1, 0, nb - 3)

    def kv_spec(col, t):
        return pl.BlockSpec((width, NA_TQ, IN_TN), lambda bi, i: (col, bi * nb + win(i) + t, 0))

    def bias_map(bi, i):
        kind = jnp.where(i == 0, 0, jnp.where(i == nb - 1, 2, 1))
        return (kind, 0, 0, 0)

    return pl.pallas_call(
        _na_kernel,
        grid=(b, nb),
        in_specs=[pl.BlockSpec((width, NA_TQ, IN_TN), lambda bi, i: (q_col, bi * nb + i, 0))]
        + [kv_spec(k_col, t) for t in range(3)]
        + [kv_spec(v_col, t) for t in range(3)]
        + [pl.BlockSpec((None, NA_HEADS, NA_TQ, NA_TK), bias_map)],
        out_specs=pl.BlockSpec((NA_TQ, NA_WIDTH), lambda bi, i: (bi * nb + i, 0)),
        out_shape=jax.ShapeDtypeStruct((b * s, NA_WIDTH), BF16),
        compiler_params=_params("parallel", "arbitrary"),
        name="na_attn",
    )(qkv, qkv, qkv, qkv, qkv, qkv, qkv, bias)


def _rope_tables(s):
    half = MLA_ROPE // 2
    inv = ROPE_THETA ** (-jnp.arange(half, dtype=F32) / half)
    ang = jnp.arange(s).astype(F32)[:, None] * inv[None, :]
    cos, sin = jnp.cos(ang), jnp.sin(ang)
    zero = jnp.zeros((s, LANES - MLA_ROPE), F32)
    return (jnp.concatenate([cos, cos, zero], axis=-1), jnp.concatenate([-sin, sin, zero], axis=-1))


def _mla_prep_kernel(cq_ref, ckv_ref, kr_ref, krr_ref, cos_ref, sin_ref, cqg_ref, ckvg_ref, wuq_ref, wukv_ref,
                     qg_ref, kg_ref, q_out, k_out, v_out):
    cos = cos_ref[...]
    sin = sin_ref[...]
    qf = _dot(_rms(cq_ref[...], cqg_ref[...]).astype(BF16), wuq_ref[...])
    kvf = _dot(_rms(ckv_ref[...], ckvg_ref[...]).astype(BF16), wukv_ref[...])
    kr = kr_ref[...] * cos + krr_ref[...] * sin
    kr_ss = jnp.sum(kr * kr, axis=-1, keepdims=True)
    qg = qg_ref[...]
    kg = kg_ref[...]
    for h in range(MLA_HEADS):
        qb = h * MLA_QEXT
        qn = qf[:, qb:qb + LANES]
        qr = qf[:, qb + LANES:qb + 2 * LANES] * cos + qf[:, qb + 2 * LANES:qb + 3 * LANES] * sin
        ms = (jnp.sum(qn * qn, axis=-1, keepdims=True) + jnp.sum(qr * qr, axis=-1, keepdims=True)) * (1.0 / MLA_QK)
        r = lax.rsqrt(ms + EPS)
        ob = h * MLA_QPAD
        q_out[:, ob:ob + LANES] = (qn * r * qg[:, :LANES]).astype(BF16)
        q_out[:, ob + LANES:ob + 2 * LANES] = (qr * r * qg[:, LANES:]).astype(BF16)
        kb = h * (MLA_NOPE + MLA_V)
        kn = kvf[:, kb:kb + MLA_NOPE]
        msk = (jnp.sum(kn * kn, axis=-1, keepdims=True) + kr_ss) * (1.0 / MLA_QK)
        rk = lax.rsqrt(msk + EPS)
        k_out[:, ob:ob + LANES] = (kn * rk * kg[:, :LANES]).astype(BF16)
        k_out[:, ob + LANES:ob + 2 * LANES] = (kr * rk * kg[:, LANES:]).astype(BF16)
        v_out[:, h * MLA_V:(h + 1) * MLA_V] = kvf[:, kb + MLA_NOPE:kb + MLA_NOPE + MLA_V].astype(BF16)


def _mla_prep(aux, cos, sin, cq_gain, ckv_gain, wuq, wukv, q_gain, k_gain, layer, s, *, tm=256):
    t = aux.shape[0]
    n_pos = s // tm
    full = lambda shape: pl.BlockSpec((None,) + shape, lambda i: (layer,) + (0,) * len(shape))
    return pl.pallas_call(
        _mla_prep_kernel,
        grid=(t // tm,),
        in_specs=[
            pl.BlockSpec((tm, MLA_Q_RANK), lambda i: (i, 0)),
            pl.BlockSpec((tm, MLA_KV_RANK), lambda i: (i, MLA_Q_RANK // MLA_KV_RANK)),
            pl.BlockSpec((tm, LANES), lambda i: (i, (MLA_Q_RANK + MLA_KV_RANK) // LANES)),
            pl.BlockSpec((tm, LANES), lambda i: (i, (MLA_Q_RANK + MLA_KV_RANK) // LANES + 1)),
            pl.BlockSpec((tm, LANES), lambda i: (i % n_pos, 0)),
            pl.BlockSpec((tm, LANES), lambda i: (i % n_pos, 0)),
            full((1, MLA_Q_RANK)),
            full((1, MLA_KV_RANK)),
            full((MLA_Q_RANK, MLA_HEADS * MLA_QEXT)),
            full((MLA_KV_RANK, MLA_HEADS * (MLA_NOPE + MLA_V))),
            full((1, MLA_QPAD)),
            full((1, MLA_QPAD)),
        ],
        out_specs=[
            pl.BlockSpec((tm, MLA_HEADS * MLA_QPAD), lambda i: (i, 0)),
            pl.BlockSpec((tm, MLA_HEADS * MLA_QPAD), lambda i: (i, 0)),
            pl.BlockSpec((tm, MLA_OUT), lambda i: (i, 0)),
        ],
        out_shape=[
            jax.ShapeDtypeStruct((t, MLA_HEADS * MLA_QPAD), BF16),
            jax.ShapeDtypeStruct((t, MLA_HEADS * MLA_QPAD), BF16),
            jax.ShapeDtypeStruct((t, MLA_OUT), BF16),
        ],
        compiler_params=_params("parallel"),
        name="mla_prep",
    )(aux, aux, aux, aux, cos, sin, cq_gain, ckv_gain, wuq, wukv, q_gain, k_gain)


MLA_TQ = 1024
MLA_KCHUNK = 512


def _lane_groups(x, op):
    out = x[:, :LANES]
    for c in range(1, x.shape[1] // LANES):
        out = op(out, x[:, c * LANES:(c + 1) * LANES])
    return out


def _mla_flash_kernel(q_ref, k_ref, v_ref, o_ref, m_ref, l_ref, acc_ref):
    ki = pl.program_id(3)

    @pl.when(ki == 0)
    def _():
        m_ref[...] = jnp.full_like(m_ref, -jnp.inf)
        l_ref[...] = jnp.zeros_like(l_ref)
        acc_ref[...] = jnp.zeros_like(acc_ref)

    q = q_ref[...]
    c2 = (MLA_QK ** -0.5) * LOG2E
    m, l, acc = m_ref[...], l_ref[...], acc_ref[...]
    for c in range(k_ref.shape[0] // MLA_KCHUNK):
        ks = slice(c * MLA_KCHUNK, (c + 1) * MLA_KCHUNK)
        s = _dot_t(q, k_ref[ks, :]) * c2
        m_new = jnp.maximum(m, jnp.max(_lane_groups(s, jnp.maximum), axis=-1, keepdims=True))
        alpha = jnp.exp2(m - m_new)
        p = jnp.exp2(s - jnp.concatenate([m_new] * (MLA_KCHUNK // LANES), axis=1))
        l = alpha * l + _lane_groups(p, jnp.add)
        acc = alpha * acc + _dot(p.astype(BF16), v_ref[ks, :])
        m = m_new
    m_ref[...] = m
    l_ref[...] = l
    acc_ref[...] = acc

    @pl.when(ki == pl.num_programs(3) - 1)
    def _():
        o_ref[...] = (acc * (1.0 / jnp.sum(l, axis=-1, keepdims=True))).astype(o_ref.dtype)


def _mla_attention(q, k, v, b, s, *, tk=8192):
    tq, tk = MLA_TQ, min(tk, s)
    nq, nk = s // tq, s // tk
    stat = pltpu.VMEM((tq, LANES), F32)
    return pl.pallas_call(
        _mla_flash_kernel,
        grid=(b, MLA_HEADS, nq, nk),
        in_specs=[
            pl.BlockSpec((tq, MLA_QPAD), lambda bi, h, qi, ki: (bi * nq + qi, h)),
            pl.BlockSpec((tk, MLA_QPAD), lambda bi, h, qi, ki: (bi * nk + ki, h)),
            pl.BlockSpec((tk, MLA_V), lambda bi, h, qi, ki: (bi * nk + ki, h)),
        ],
        out_specs=pl.BlockSpec((tq, MLA_V), lambda bi, h, qi, ki: (bi * nq + qi, h)),
        out_shape=jax.ShapeDtypeStruct((b * s, MLA_OUT), BF16),
        scratch_shapes=[stat, stat, pltpu.VMEM((tq, MLA_V), F32)],
        compiler_params=_params("parallel", "parallel", "parallel", "arbitrary"),
        name="mla_flash",
    )(q, k, v)


DIL_TQ = 256
DIL_KBLOCKS = 4


def _alibi_slopes(n):
    return 2.0 ** (-8.0 * jnp.arange(1, n + 1, dtype=F32) / n)


def _dil_kernel(slope_ref, q_ref, *refs, n, radius, tq):
    k_refs, v_refs = refs[:DIL_KBLOCKS], refs[DIL_KBLOCKS:2 * DIL_KBLOCKS]
    o_ref, lse_ref = refs[2 * DIL_KBLOCKS:]
    kb = tq // 2
    tk = DIL_KBLOCKS * kb
    i = pl.program_id(2)
    row = lax.broadcasted_iota(jnp.int32, (tq, tk), 0)
    col = lax.broadcasted_iota(jnp.int32, (tq, tk), 1)
    kpos = (2 * i - 1) * kb + col
    dist_i = jnp.abs(row + kb - col)
    valid = (kpos >= 0) & (kpos < n) & (dist_i <= radius)
    dist = dist_i.astype(F32)
    scale = HEAD_DIM ** -0.5

    def window(refs, h):
        sl = slice(h * HEAD_DIM, (h + 1) * HEAD_DIM)
        return jnp.concatenate([ref[..., sl].reshape(kb, HEAD_DIM) for ref in refs], axis=0)

    def scores(h):
        q = q_ref[..., h * HEAD_DIM:(h + 1) * HEAD_DIM].reshape(tq, HEAD_DIM)
        return _dot_t(q, window(k_refs, h))

    s_next = scores(0)
    for h in range(DIL_HEADS_PER_GROUP):
        sl = slice(h * HEAD_DIM, (h + 1) * HEAD_DIM)
        s = s_next
        if h + 1 < DIL_HEADS_PER_GROUP:
            s_next = scores(h + 1)
        v = window(v_refs, h)
        s = s * scale - slope_ref[h] * dist
        s = jnp.where(valid, s, NEG_INF)
        m = jnp.max(s, axis=-1, keepdims=True)
        p = jnp.exp(s - m)
        den = jnp.sum(p, axis=-1, keepdims=True)
        o = _dot(p.astype(BF16), v) * (1.0 / den)
        o_ref[..., sl] = o.reshape(o_ref.shape[:-1] + (HEAD_DIM,))
        lse_ref[..., sl] = jnp.broadcast_to(m + jnp.log(den), (tq, HEAD_DIM)).reshape(o_ref.shape[:-1] + (HEAD_DIM,))


def _dil_group_attention(qkv, slopes, group, b, s):
    window, dil = DIL_GROUPS[group]
    radius = (window // 2) // dil
    n = s // dil
    tq = min(DIL_TQ, n)
    kb = tq // 2
    nblk = n // tq
    assert radius <= kb and s % IN_TM == 0 and n % tq == 0
    t = b * s
    rows = IN_TM // dil
    tiles = s // IN_TM
    q_unit = QKV_TILE0 + NA_TILES + 3 * group

    def spec(br, unit, pick):
        if rows >= br:
            sub = rows // br
            block = (None, None, br, DIL_OUT)
            index = lambda bi, r, u: (bi * tiles + u // sub, r, u % sub, 0)
        else:
            span = br // rows
            assert tiles % span == 0
            block = (span, None, rows, DIL_OUT)
            index = lambda bi, r, u: (bi * tiles // span + u, r, 0, 0)
        if unit is None:
            return pl.BlockSpec(block, lambda bi, r, i: index(bi, r, pick(i)))
        return pl.BlockSpec((None,) + block, lambda bi, r, i: (unit,) + index(bi, r, pick(i)))

    def kv_specs(unit):
        return [spec(kb, unit, lambda i, j=j: jnp.clip(2 * i - 1 + j, 0, n // kb - 1)) for j in range(DIL_KBLOCKS)]

    qkv_v = qkv.reshape(BF_TILES, t // IN_TM, dil, rows, IN_TN)
    kern = functools.partial(_dil_kernel, n=n, radius=radius, tq=tq)
    out_spec = spec(tq, None, lambda i: i)
    o, lse = pl.pallas_call(
        kern,
        grid=(b, dil, nblk),
        in_specs=[pl.BlockSpec(memory_space=pltpu.SMEM), spec(tq, q_unit, lambda i: i)]
        + kv_specs(q_unit + 1) + kv_specs(q_unit + 2),
        out_specs=[out_spec, out_spec],
        out_shape=[jax.ShapeDtypeStruct((t // IN_TM, dil, rows, DIL_OUT), F32)] * 2,
        compiler_params=_params("parallel", "parallel", "arbitrary"),
        name=f"dil_attn_g{group}",
    )(slopes, *([qkv_v] * (1 + 2 * DIL_KBLOCKS)))
    return o, lse


MERGE_TM = 512
NCH = DIL_OUT // LANES


def _merge_kernel(ona_ref, omla_ref, od0_ref, od1_ref, od2_ref, ls0_ref, ls1_ref, ls2_ref,
                  gna_ref, gmla_ref, gdil_ref, wna_ref, wmla_ref, wdil_ref, o_ref, od_ref, tok_ref):
    tm = MERGE_TM
    for a, (ref, dil) in enumerate(((od1_ref, 4), (ls1_ref, 4), (od2_ref, 16), (ls2_ref, 16))):
        for r in range(dil):
            for c in range(NCH):
                tok_ref[a, c, pl.ds(r, tm // dil, stride=dil), :] = ref[r, :, c * LANES:(c + 1) * LANES]
    for c in range(NCH):
        cs = slice(c * LANES, (c + 1) * LANES)
        od0, ls0 = od0_ref[0, :, cs], ls0_ref[0, :, cs]
        od1, ls1, od2, ls2 = tok_ref[0, c], tok_ref[1, c], tok_ref[2, c], tok_ref[3, c]
        top = jnp.maximum(jnp.maximum(ls0, ls1), ls2)
        w0, w1, w2 = jnp.exp(ls0 - top), jnp.exp(ls1 - top), jnp.exp(ls2 - top)
        inv = 1.0 / (w0 + w1 + w2)
        od = (w0 * inv) * od0 + (w1 * inv) * od1 + (w2 * inv) * od2
        od_ref[:, cs] = od.astype(BF16)

    o_na, o_mla, o_dil = ona_ref[...], omla_ref[...], od_ref[...]
    tn = wna_ref.shape[-1]
    for j in range(wna_ref.shape[0]):
        merged = gna_ref[j].astype(F32) * _dot(o_na, wna_ref[j])
        merged += gmla_ref[j].astype(F32) * _dot(o_mla, wmla_ref[j])
        merged += gdil_ref[j].astype(F32) * _dot(o_dil, wdil_ref[j])
        o_ref[:, j * tn:(j + 1) * tn] = merged.astype(o_ref.dtype)


def _col_tiles(w, tn):
    depth, k, n = w.shape
    return w.reshape(depth, k, n // tn, tn).transpose(0, 2, 1, 3)


def _merge(o_na, o_mla, o_dil, lse_dil, gates, w_na, w_mla, w_dil, layer):
    t = o_na.shape[0]
    tm = MERGE_TM
    tn = w_na.shape[-1]
    halves = IN_TM // tm
    gstep = D_MODEL // tn
    row = lambda width: pl.BlockSpec((tm, width), lambda i: (i, 0))
    gate = lambda g: pl.BlockSpec((gstep, tm, tn), lambda i: (g, i, 0))
    weight = lambda w: pl.BlockSpec((None,) + w.shape[1:], lambda i: (layer, 0, 0, 0), pipeline_mode=pl.Buffered(1))

    def dil_spec(g):
        dil = DIL_GROUPS[g][1]
        return pl.BlockSpec((None, dil, tm // dil, DIL_OUT), lambda i: (i // halves, 0, i % halves, 0))

    return pl.pallas_call(
        _merge_kernel,
        grid=(t // tm,),
        in_specs=[row(NA_WIDTH), row(MLA_OUT)] + [dil_spec(g) for g in range(3)] * 2
        + [gate(0), gate(1), gate(2)] + [weight(w) for w in (w_na, w_mla, w_dil)],
        out_specs=pl.BlockSpec((tm, D_MODEL), lambda i: (i, 0)),
        out_shape=jax.ShapeDtypeStruct((t, D_MODEL), BF16),
        scratch_shapes=[pltpu.VMEM((tm, DIL_OUT), BF16), pltpu.VMEM((4, NCH, tm, LANES), F32)],
        compiler_params=_params("parallel"),
        name="merge",
    )(o_na, o_mla, *o_dil, *lse_dil, gates, gates, gates, w_na, w_mla, w_dil)


def _resid_matmul_kernel(a_ref, w_ref, x_ref, o_ref):
    a = a_ref[...]
    tn = w_ref.shape[-1]
    for j in range(w_ref.shape[0]):
        cs = slice(j * tn, (j + 1) * tn)
        o_ref[:, cs] = x_ref[:, cs] + _dot(a, w_ref[j])


def _resid_matmul(a, w, x, layer, *, tm=1024):
    t, k = a.shape
    n = w.shape[1] * w.shape[-1]
    return pl.pallas_call(
        _resid_matmul_kernel,
        grid=(t // tm,),
        in_specs=[
            pl.BlockSpec((tm, k), lambda i: (i, 0)),
            pl.BlockSpec((None,) + w.shape[1:], lambda i: (layer, 0, 0, 0), pipeline_mode=pl.Buffered(1)),
            pl.BlockSpec((tm, n), lambda i: (i, 0)),
        ],
        out_specs=pl.BlockSpec((tm, n), lambda i: (i, 0)),
        out_shape=jax.ShapeDtypeStruct((t, n), F32),
        compiler_params=_params("parallel"),
        name="wo_resid",
    )(a, w, x)


def _prepare(p):
    depth = p["w_in"].shape[0]
    row = lambda g: g[:, None, :].astype(F32)
    w_in = p["w_in"]
    o = np.cumsum((0, NA_WIDTH, NA_WIDTH, NA_WIDTH, MLA_Q_RANK, MLA_KV_RANK, MLA_ROPE,
                   DIL_WIDTH, DIL_WIDTH, DIL_WIDTH, D_MODEL, D_MODEL, D_MODEL))
    seg = lambda a: w_in[:, :, o[a]:o[a + 1]]
    grp = lambda a, g: seg(a)[:, :, g * DIL_OUT:(g + 1) * DIL_OUT]
    rot = np.concatenate([np.arange(MLA_ROPE // 2, MLA_ROPE), np.arange(MLA_ROPE // 2)])
    zpad = jnp.zeros((depth, D_MODEL, LANES - MLA_ROPE), w_in.dtype)
    cols = [seg(9), seg(10), seg(11), seg(0), seg(1), seg(2)]
    for g in range(len(DIL_GROUPS)):
        cols += [grp(6, g), grp(7, g), grp(8, g)]
    cols += [seg(3), seg(4), seg(5), zpad, seg(5)[:, :, rot], zpad]
    w_all = jnp.concatenate(cols, axis=-1).astype(BF16)

    tile = lambda g, reps: jnp.tile(g, (1, reps))
    ones = lambda n: jnp.ones((depth, n), F32)
    dil_gain = jnp.concatenate([tile(p["dil_q_norm"], DIL_HEADS_PER_GROUP), tile(p["dil_k_norm"], DIL_HEADS_PER_GROUP),
                                ones(DIL_OUT)], axis=-1)
    head_gain = jnp.concatenate(
        [tile(p["na_q_norm"], NA_HEADS), tile(p["na_k_norm"], NA_HEADS), ones(NA_WIDTH)]
        + [dil_gain] * len(DIL_GROUPS), axis=-1)

    wuq = p["mla_w_uq"].reshape(depth, MLA_Q_RANK, MLA_HEADS, MLA_QK)
    zq = jnp.zeros((depth, MLA_Q_RANK, MLA_HEADS, LANES - MLA_ROPE), wuq.dtype)
    wuq_rope = wuq[..., MLA_NOPE:]
    wuq = jnp.concatenate([wuq[..., :MLA_NOPE], wuq_rope, zq, wuq_rope[..., rot], zq], axis=-1)
    pad_gain = lambda g: jnp.concatenate([g, jnp.zeros((depth, MLA_QPAD - MLA_QK), F32)], axis=-1)

    return dict(
        ffn1=(row(p["ffn1_norm"]), p["ffn1_w_gate"].astype(BF16), p["ffn1_w_up"].astype(BF16),
              p["ffn1_w_down"].astype(BF16)),
        ffn2=(row(p["ffn2_norm"]), p["ffn2_w_gate"].astype(BF16), p["ffn2_w_up"].astype(BF16),
              p["ffn2_w_down"].astype(BF16)),
        mix_norm=row(p["mix_norm"]),
        w_all=w_all, head_gain=row(head_gain),
        na_bias=[_na_bias_tables(p["na_rpb"][l]) for l in range(depth)],
        cq_gain=row(p["mla_cq_norm"]), ckv_gain=row(p["mla_ckv_norm"]),
        wuq=wuq.reshape(depth, MLA_Q_RANK, MLA_HEADS * MLA_QEXT).astype(BF16),
        wukv=p["mla_w_ukv"].astype(BF16),
        q_gain=row(pad_gain(p["mla_q_norm"])), k_gain=row(pad_gain(p["mla_k_norm"])),
        w_na=_col_tiles(p["w_na_out"].astype(BF16), IN_TN), w_mla=_col_tiles(p["w_mla_out"].astype(BF16), IN_TN),
        w_dil=_col_tiles(p["w_dil_out"].astype(BF16), IN_TN), w_o=_col_tiles(p["w_o"].astype(BF16), IN_TN),
    )


def _dil_slopes():
    slopes = _alibi_slopes(DIL_HEADS)
    out = []
    for g, (_, dil) in enumerate(DIL_GROUPS):
        lo, hi = g * DIL_HEADS_PER_GROUP, (g + 1) * DIL_HEADS_PER_GROUP
        out.append(slopes[lo:hi] * dil)
    return out


def _layer(x, w, layer, b, s, rope, slopes):
    x = _ffn(x, *w["ffn1"], layer)
    qkv, aux = _inproj(x, w["mix_norm"], w["w_all"], w["head_gain"], layer)
    o_na = _na_attention(qkv, w["na_bias"][layer], b, s)
    q, k, v = _mla_prep(aux, rope[0], rope[1], w["cq_gain"], w["ckv_gain"], w["wuq"], w["wukv"],
                        w["q_gain"], w["k_gain"], layer, s)
    o_mla = _mla_attention(q, k, v, b, s)
    dil = [_dil_group_attention(qkv, slopes[g], g, b, s) for g in range(len(DIL_GROUPS))]
    merged = _merge(o_na, o_mla, [d[0] for d in dil], [d[1] for d in dil], qkv,
                    w["w_na"], w["w_mla"], w["w_dil"], layer)
    x = _resid_matmul(merged, w["w_o"], x, layer)
    return _ffn(x, *w["ffn2"], layer)


def _trunk(x, w, slopes):
    b, s, d = x.shape
    rope = _rope_tables(s)
    y = x.reshape(b * s, d)
    for layer in range(w["w_all"].shape[0]):
        y = _layer(y, w, layer, b, s, rope, slopes)
    return y.reshape(b, s, d)


def kernel(x_prompt, x_sample, ffn1_norm, ffn1_w_gate, ffn1_w_up, ffn1_w_down, mix_norm, w_in, na_q_norm, na_k_norm, na_rpb, mla_cq_norm, mla_ckv_norm, mla_w_uq, mla_w_ukv, mla_q_norm, mla_k_norm, dil_q_norm, dil_k_norm, w_na_out, w_mla_out, w_dil_out, w_o, ffn2_norm, ffn2_w_gate, ffn2_w_up, ffn2_w_down):
    w = _prepare(dict(
        ffn1_norm=ffn1_norm, ffn1_w_gate=ffn1_w_gate, ffn1_w_up=ffn1_w_up, ffn1_w_down=ffn1_w_down,
        mix_norm=mix_norm, w_in=w_in, na_q_norm=na_q_norm, na_k_norm=na_k_norm, na_rpb=na_rpb,
        mla_cq_norm=mla_cq_norm, mla_ckv_norm=mla_ckv_norm, mla_w_uq=mla_w_uq, mla_w_ukv=mla_w_ukv,
        mla_q_norm=mla_q_norm, mla_k_norm=mla_k_norm, dil_q_norm=dil_q_norm, dil_k_norm=dil_k_norm,
        w_na_out=w_na_out, w_mla_out=w_mla_out, w_dil_out=w_dil_out, w_o=w_o,
        ffn2_norm=ffn2_norm, ffn2_w_gate=ffn2_w_gate, ffn2_w_up=ffn2_w_up, ffn2_w_down=ffn2_w_down))
    slopes = _dil_slopes()
    return (_trunk(x_prompt, w, slopes), _trunk(x_sample, w, slopes))
```

```python
import functools

import numpy as np
import jax
import jax.numpy as jnp
from jax import lax
from jax.experimental import pallas as pl
from jax.experimental.pallas import tpu as pltpu

F32 = jnp.float32
BF16 = jnp.bfloat16

D_MODEL = 2048
GRID_W = 64
HEAD_DIM = 128
EPS = 1e-6
NEG_INF = -1e30
LOG2E = 1.4426950408889634

NA_HEADS = 8
NA_WIN_R = 8
NA_WIN_C = 16
NA_WIDTH = NA_HEADS * HEAD_DIM

MLA_HEADS = 8
MLA_Q_RANK = 512
MLA_KV_RANK = 256
MLA_NOPE = 128
MLA_ROPE = 64
MLA_V = 128
MLA_QK = MLA_NOPE + MLA_ROPE
MLA_OUT = MLA_HEADS * MLA_V
ROPE_THETA = 10000.0

DIL_GROUPS = ((128, 1), (512, 4), (2048, 16))
DIL_HEADS_PER_GROUP = 4
DIL_HEADS = 12
DIL_WIDTH = DIL_HEADS * HEAD_DIM
DIL_OUT = DIL_HEADS_PER_GROUP * HEAD_DIM

D_FF = 5632
FFN_TF = 512

LANES = 128
VMEM_LIMIT = 56 * 1024 * 1024

IN_TM = 1024
IN_TN = 512
GATE_TILES = 3 * D_MODEL // IN_TN
QKV_TILE0 = GATE_TILES
QKV_TILES = (3 * NA_WIDTH + 3 * DIL_WIDTH) // IN_TN
NA_TILES = 3 * NA_WIDTH // IN_TN
BF_TILES = GATE_TILES + QKV_TILES
BF_COLS = BF_TILES * IN_TN
LAT_COLS = MLA_Q_RANK + MLA_KV_RANK + 2 * LANES
LAT_TILES = LAT_COLS // IN_TN
MLA_QPAD = 2 * LANES
MLA_QEXT = 3 * LANES


def _params(*sem):
    return pltpu.CompilerParams(dimension_semantics=sem, vmem_limit_bytes=VMEM_LIMIT)


def _rms(x, gain):
    return x * lax.rsqrt(jnp.mean(x * x, axis=-1, keepdims=True) + EPS) * gain


def _dot(a, b):
    return jnp.dot(a, b, preferred_element_type=F32)


def _dot_t(a, b):
    return lax.dot_general(a, b, (((1,), (1,)), ((), ())), preferred_element_type=F32)


def _ffn_kernel(x_ref, g_ref, wg_ref, wu_ref, wd_ref, o_ref, xn_ref):
    f = pl.program_id(1)

    @pl.when(f == 0)
    def _():
        x = x_ref[...]
        xn_ref[...] = _rms(x, g_ref[...]).astype(BF16)
        o_ref[...] = x

    xn = xn_ref[...]
    g = _dot(xn, wg_ref[...])
    u = _dot(xn, wu_ref[...])
    h = (g * jax.nn.sigmoid(g) * u * 0.5).astype(BF16)
    o_ref[...] += _dot(h, wd_ref[...])


def _ffn(x, gain, wg, wu, wd, layer, *, tm=1024, tf=FFN_TF):
    t, d = x.shape
    ff = wg.shape[-1]
    return pl.pallas_call(
        _ffn_kernel,
        grid=(t // tm, ff // tf),
        in_specs=[
            pl.BlockSpec((tm, d), lambda i, f: (i, 0)),
            pl.BlockSpec((None, 1, d), lambda i, f: (layer, 0, 0)),
            pl.BlockSpec((None, d, tf), lambda i, f: (layer, 0, f)),
            pl.BlockSpec((None, d, tf), lambda i, f: (layer, 0, f)),
            pl.BlockSpec((None, tf, d), lambda i, f: (layer, f, 0)),
        ],
        out_specs=pl.BlockSpec((tm, d), lambda i, f: (i, 0)),
        out_shape=jax.ShapeDtypeStruct((t, d), F32),
        scratch_shapes=[pltpu.VMEM((tm, d), BF16)],
        compiler_params=_params("parallel", "arbitrary"),
        name="ffn",
    )(x, gain, wg, wu, wd)


def _inproj_kernel(x_ref, g_ref, w_ref, hg_ref, qkv_ref, aux_ref, yn_ref, xn_ref):
    j = pl.program_id(1)
    tm, d = x_ref.shape

    @pl.when(j == 0)
    def _():
        x = x_ref[...]
        inv = lax.rsqrt(jnp.mean(x * x, axis=-1, keepdims=True) + EPS)
        for c in range(d // LANES):
            cs = slice(c * LANES, (c + 1) * LANES)
            y = x_ref[:, cs] * inv * g_ref[:, cs]
            yn_ref[c] = y
            xn_ref[0, :, cs] = y.astype(BF16)
        for g in (1, 2):
            dil = DIL_GROUPS[g][1]
            rows = tm // dil
            for r in range(dil):
                for c in range(d // LANES):
                    xn_ref[g, r * rows:(r + 1) * rows, c * LANES:(c + 1) * LANES] = (
                        yn_ref[c, pl.ds(r, rows, stride=dil), :].astype(BF16))

    jq = j - QKV_TILE0
    is_gate = j < QKV_TILE0
    is_qkv = (jq >= 0) & (jq < QKV_TILES)
    normed = is_qkv & ((jq < 4) | ((jq >= NA_TILES) & ((jq - NA_TILES) % 3 != 2)))
    order = jnp.where(jq < NA_TILES + 3, 0, jnp.where(jq < NA_TILES + 6, 1, 2))

    @pl.when(is_gate)
    def _():
        qkv_ref[...] = jax.nn.sigmoid(_dot(xn_ref[0], w_ref[...])).astype(BF16)

    @pl.when(normed)
    def _():
        acc = _dot(xn_ref[order], w_ref[...])
        hg = hg_ref[...]
        for c in range(acc.shape[1] // HEAD_DIM):
            sl = slice(c * HEAD_DIM, (c + 1) * HEAD_DIM)
            qkv_ref[:, sl] = _rms(acc[:, sl], hg[:, sl]).astype(BF16)

    @pl.when(is_qkv & jnp.logical_not(normed))
    def _():
        qkv_ref[...] = _dot(xn_ref[order], w_ref[...]).astype(BF16)

    @pl.when(j >= BF_TILES)
    def _():
        aux_ref[...] = _dot(xn_ref[0], w_ref[...])


def _inproj(x, gain, w, head_gain, layer):
    t, d = x.shape
    tm, tn = IN_TM, IN_TN
    return pl.pallas_call(
        _inproj_kernel,
        grid=(t // tm, BF_TILES + LAT_TILES),
        in_specs=[
            pl.BlockSpec((tm, d), lambda i, j: (i, 0)),
            pl.BlockSpec((None, 1, d), lambda i, j: (layer, 0, 0)),
            pl.BlockSpec((None, d, tn), lambda i, j: (layer, 0, j)),
            pl.BlockSpec((None, 1, tn), lambda i, j: (layer, 0, jnp.clip(j - QKV_TILE0, 0, QKV_TILES - 1))),
        ],
        out_specs=[
            pl.BlockSpec((None, tm, tn), lambda i, j: (jnp.minimum(j, BF_TILES - 1), i, 0)),
            pl.BlockSpec((tm, tn), lambda i, j: (i, jnp.maximum(j - BF_TILES, 0))),
        ],
        out_shape=[jax.ShapeDtypeStruct((BF_TILES, t, tn), BF16), jax.ShapeDtypeStruct((t, LAT_COLS), F32)],
        scratch_shapes=[pltpu.VMEM((d // LANES, tm, LANES), F32), pltpu.VMEM((3, tm, d), BF16)],
        compiler_params=_params("parallel", "arbitrary"),
        name="inproj",
    )(x, gain, w, head_gain)


NA_QROWS = 4
NA_TQ = NA_QROWS * GRID_W
NA_KROWS = 3 * NA_QROWS
NA_TK = NA_KROWS * GRID_W


def _na_bias_tables(rpb):
    heads = rpb.shape[0]
    qc = np.arange(GRID_W)[:, None]
    kc = np.arange(GRID_W)[None, :]
    col_start = np.clip(qc - NA_WIN_C // 2, 0, GRID_W - NA_WIN_C)
    col_ok = (kc >= col_start) & (kc < col_start + NA_WIN_C)
    dc_idx = np.clip(kc - qc, -(NA_WIN_C - 1), NA_WIN_C - 1) + NA_WIN_C - 1
    onehot = (dc_idx[None] == np.arange(2 * NA_WIN_C - 1)[:, None, None]).astype(np.float32)
    by_col = jnp.einsum("hrd,dqk->hqrk", rpb, jnp.asarray(onehot), precision=lax.Precision.HIGHEST)
    by_col = jnp.where(jnp.asarray(col_ok)[None, :, None, :], by_col, NEG_INF)
    tables = []
    for q_rel, k0_rel in ((0, None), (NA_QROWS, 0), (2 * NA_QROWS, None)):
        rows = []
        for j in range(NA_QROWS):
            k0 = j if k0_rel == 0 else (0 if q_rel == 0 else NA_QROWS)
            dr0 = k0 - (q_rel + j) + NA_WIN_R - 1
            win = by_col[:, :, dr0:dr0 + NA_WIN_R, :]
            pad = lambda n: jnp.full((heads, GRID_W, n, GRID_W), NEG_INF, F32)
            rows.append(jnp.concatenate([pad(k0), win, pad(NA_KROWS - NA_WIN_R - k0)], axis=2))
        tables.append(jnp.stack(rows, axis=1).reshape(heads, NA_TQ, NA_TK))
    return jnp.stack(tables)


def _head_cols(h):
    per_tile = IN_TN // HEAD_DIM
    return h // per_tile, slice((h % per_tile) * HEAD_DIM, (h % per_tile + 1) * HEAD_DIM)


def _na_kernel(q_ref, k0_ref, k1_ref, k2_ref, v0_ref, v1_ref, v2_ref, bias_ref, o_ref):
    scale = HEAD_DIM ** -0.5

    def window(refs, h):
        tile, hs = _head_cols(h)
        return jnp.concatenate([ref[tile, :, hs] for ref in refs], axis=0)

    def scores(h):
        tile, hs = _head_cols(h)
        return _dot_t(q_ref[tile, :, hs], window((k0_ref, k1_ref, k2_ref), h))

    s_next = scores(0)
    for h in range(NA_HEADS):
        s = s_next
        if h + 1 < NA_HEADS:
            s_next = scores(h + 1)
        s = s * scale + bias_ref[h]
        m = jnp.max(s, axis=-1, keepdims=True)
        e = jnp.exp(s - m)
        den = jnp.sum(e, axis=-1, keepdims=True)
        o = _dot(e.astype(BF16), window((v0_ref, v1_ref, v2_ref), h)) * (1.0 / den)
        o_ref[:, h * HEAD_DIM:(h + 1) * HEAD_DIM] = o.astype(o_ref.dtype)


def _na_attention(qkv, bias, b, s):
    nb = s // NA_TQ
    assert nb >= 3
    width = NA_WIDTH // IN_TN
    q_col = QKV_TILE0 // width
    k_col, v_col = q_col + 1, q_col + 2

    def win(i):
        return jnp.clip(i - 1, 0, nb - 3)

    def kv_spec(col, t):
        return pl.BlockSpec((width, NA_TQ, IN_TN), lambda bi, i: (col, bi * nb + win(i) + t, 0))

    def bias_map(bi, i):
        kind = jnp.where(i == 0, 0, jnp.where(i == nb - 1, 2, 1))
        return (kind, 0, 0, 0)

    return pl.pallas_call(
        _na_kernel,
        grid=(b, nb),
        in_specs=[pl.BlockSpec((width, NA_TQ, IN_TN), lambda bi, i: (q_col, bi * nb + i, 0))]
        + [kv_spec(k_col, t) for t in range(3)]
        + [kv_spec(v_col, t) for t in range(3)]
        + [pl.BlockSpec((None, NA_HEADS, NA_TQ, NA_TK), bias_map)],
        out_specs=pl.BlockSpec((NA_TQ, NA_WIDTH), lambda bi, i: (bi * nb + i, 0)),
        out_shape=jax.ShapeDtypeStruct((b * s, NA_WIDTH), BF16),
        compiler_params=_params("parallel", "arbitrary"),
        name="na_attn",
    )(qkv, qkv, qkv, qkv, qkv, qkv, qkv, bias)


def _rope_tables(s):
    half = MLA_ROPE // 2
    inv = ROPE_THETA ** (-jnp.arange(half, dtype=F32) / half)
    ang = jnp.arange(s).astype(F32)[:, None] * inv[None, :]
    cos, sin = jnp.cos(ang), jnp.sin(ang)
    zero = jnp.zeros((s, LANES - MLA_ROPE), F32)
    return (jnp.concatenate([cos, cos, zero], axis=-1), jnp.concatenate([-sin, sin, zero], axis=-1))


def _mla_prep_kernel(cq_ref, ckv_ref, kr_ref, krr_ref, cos_ref, sin_ref, cqg_ref, ckvg_ref, wuq_ref, wukv_ref,
                     qg_ref, kg_ref, q_out, k_out, v_out):
    cos = cos_ref[...]
    sin = sin_ref[...]
    qf = _dot(_rms(cq_ref[...], cqg_ref[...]).astype(BF16), wuq_ref[...])
    kvf = _dot(_rms(ckv_ref[...], ckvg_ref[...]).astype(BF16), wukv_ref[...])
    kr = kr_ref[...] * cos + krr_ref[...] * sin
    kr_ss = jnp.sum(kr * kr, axis=-1, keepdims=True)
    qg = qg_ref[...]
    kg = kg_ref[...]
    for h in range(MLA_HEADS):
        qb = h * MLA_QEXT
        qn = qf[:, qb:qb + LANES]
        qr = qf[:, qb + LANES:qb + 2 * LANES] * cos + qf[:, qb + 2 * LANES:qb + 3 * LANES] * sin
        ms = (jnp.sum(qn * qn, axis=-1, keepdims=True) + jnp.sum(qr * qr, axis=-1, keepdims=True)) * (1.0 / MLA_QK)
        r = lax.rsqrt(ms + EPS)
        ob = h * MLA_QPAD
        q_out[:, ob:ob + LANES] = (qn * r * qg[:, :LANES]).astype(BF16)
        q_out[:, ob + LANES:ob + 2 * LANES] = (qr * r * qg[:, LANES:]).astype(BF16)
        kb = h * (MLA_NOPE + MLA_V)
        kn = kvf[:, kb:kb + MLA_NOPE]
        msk = (jnp.sum(kn * kn, axis=-1, keepdims=True) + kr_ss) * (1.0 / MLA_QK)
        rk = lax.rsqrt(msk + EPS)
        k_out[:, ob:ob + LANES] = (kn * rk * kg[:, :LANES]).astype(BF16)
        k_out[:, ob + LANES:ob + 2 * LANES] = (kr * rk * kg[:, LANES:]).astype(BF16)
        v_out[:, h * MLA_V:(h + 1) * MLA_V] = kvf[:, kb + MLA_NOPE:kb + MLA_NOPE + MLA_V].astype(BF16)


def _mla_prep(aux, cos, sin, cq_gain, ckv_gain, wuq, wukv, q_gain, k_gain, layer, s, *, tm=512):
    t = aux.shape[0]
    n_pos = s // tm
    full = lambda shape: pl.BlockSpec((None,) + shape, lambda i: (layer,) + (0,) * len(shape))
    return pl.pallas_call(
        _mla_prep_kernel,
        grid=(t // tm,),
        in_specs=[
            pl.BlockSpec((tm, MLA_Q_RANK), lambda i: (i, 0)),
            pl.BlockSpec((tm, MLA_KV_RANK), lambda i: (i, MLA_Q_RANK // MLA_KV_RANK)),
            pl.BlockSpec((tm, LANES), lambda i: (i, (MLA_Q_RANK + MLA_KV_RANK) // LANES)),
            pl.BlockSpec((tm, LANES), lambda i: (i, (MLA_Q_RANK + MLA_KV_RANK) // LANES + 1)),
            pl.BlockSpec((tm, LANES), lambda i: (i % n_pos, 0)),
            pl.BlockSpec((tm, LANES), lambda i: (i % n_pos, 0)),
            full((1, MLA_Q_RANK)),
            full((1, MLA_KV_RANK)),
            full((MLA_Q_RANK, MLA_HEADS * MLA_QEXT)),
            full((MLA_KV_RANK, MLA_HEADS * (MLA_NOPE + MLA_V))),
            full((1, MLA_QPAD)),
            full((1, MLA_QPAD)),
        ],
        out_specs=[
            pl.BlockSpec((tm, MLA_HEADS * MLA_QPAD), lambda i: (i, 0)),
            pl.BlockSpec((tm, MLA_HEADS * MLA_QPAD), lambda i: (i, 0)),
            pl.BlockSpec((tm, MLA_OUT), lambda i: (i, 0)),
        ],
        out_shape=[
            jax.ShapeDtypeStruct((t, MLA_HEADS * MLA_QPAD), BF16),
            jax.ShapeDtypeStruct((t, MLA_HEADS * MLA_QPAD), BF16),
            jax.ShapeDtypeStruct((t, MLA_OUT), BF16),
        ],
        compiler_params=_params("parallel"),
        name="mla_prep",
    )(aux, aux, aux, aux, cos, sin, cq_gain, ckv_gain, wuq, wukv, q_gain, k_gain)


MLA_TQ = 1024
MLA_KCHUNK = 512


def _lane_groups(x, op):
    out = x[:, :LANES]
    for c in range(1, x.shape[1] // LANES):
        out = op(out, x[:, c * LANES:(c + 1) * LANES])
    return out


def _mla_flash_kernel(q_ref, k_ref, v_ref, o_ref, m_ref, l_ref, acc_ref):
    ki = pl.program_id(3)

    @pl.when(ki == 0)
    def _():
        m_ref[...] = jnp.full_like(m_ref, -jnp.inf)
        l_ref[...] = jnp.zeros_like(l_ref)
        acc_ref[...] = jnp.zeros_like(acc_ref)

    q = q_ref[...]
    m, l, acc = m_ref[...], l_ref[...], acc_ref[...]
    for c in range(k_ref.shape[0] // MLA_KCHUNK):
        ks = slice(c * MLA_KCHUNK, (c + 1) * MLA_KCHUNK)
        s = _dot_t(q, k_ref[ks, :])
        m_new = jnp.maximum(m, jnp.max(_lane_groups(s, jnp.maximum), axis=-1, keepdims=True))
        alpha = jnp.exp2(m - m_new)
        p = jnp.exp2(s - jnp.concatenate([m_new] * (MLA_KCHUNK // LANES), axis=1))
        l = alpha * l + _lane_groups(p, jnp.add)
        acc = alpha * acc + _dot(p.astype(BF16), v_ref[ks, :])
        m = m_new
    m_ref[...] = m
    l_ref[...] = l
    acc_ref[...] = acc

    @pl.when(ki == pl.num_programs(3) - 1)
    def _():
        o_ref[...] = (acc * (1.0 / jnp.sum(l, axis=-1, keepdims=True))).astype(o_ref.dtype)


def _mla_attention(q, k, v, b, s, *, tk=8192):
    tq, tk = MLA_TQ, min(tk, s)
    nq, nk = s // tq, s // tk
    stat = pltpu.VMEM((tq, LANES), F32)
    return pl.pallas_call(
        _mla_flash_kernel,
        grid=(b, MLA_HEADS, nq, nk),
        in_specs=[
            pl.BlockSpec((tq, MLA_QPAD), lambda bi, h, qi, ki: (bi * nq + qi, h)),
            pl.BlockSpec((tk, MLA_QPAD), lambda bi, h, qi, ki: (bi * nk + ki, h)),
            pl.BlockSpec((tk, MLA_V), lambda bi, h, qi, ki: (bi * nk + ki, h)),
        ],
        out_specs=pl.BlockSpec((tq, MLA_V), lambda bi, h, qi, ki: (bi * nq + qi, h)),
        out_shape=jax.ShapeDtypeStruct((b * s, MLA_OUT), BF16),
        scratch_shapes=[stat, stat, pltpu.VMEM((tq, MLA_V), F32)],
        compiler_params=_params("parallel", "parallel", "parallel", "arbitrary"),
        name="mla_flash",
    )(q, k, v)


DIL_TQ = 256
DIL_KBLOCKS = 4


def _alibi_slopes(n):
    return 2.0 ** (-8.0 * jnp.arange(1, n + 1, dtype=F32) / n)


def _dil_kernel(slope_ref, q_ref, *refs, n, radius, tq):
    k_refs, v_refs = refs[:DIL_KBLOCKS], refs[DIL_KBLOCKS:2 * DIL_KBLOCKS]
    o_ref, lse_ref = refs[2 * DIL_KBLOCKS:]
    kb = tq // 2
    tk = DIL_KBLOCKS * kb
    i = pl.program_id(2)
    row = lax.broadcasted_iota(jnp.int32, (tq, tk), 0)
    col = lax.broadcasted_iota(jnp.int32, (tq, tk), 1)
    kpos = (2 * i - 1) * kb + col
    dist_i = jnp.abs(row + kb - col)
    valid = (kpos >= 0) & (kpos < n) & (dist_i <= radius)
    dist = dist_i.astype(F32)
    scale = HEAD_DIM ** -0.5

    def window(refs, h):
        sl = slice(h * HEAD_DIM, (h + 1) * HEAD_DIM)
        return jnp.concatenate([ref[..., sl].reshape(kb, HEAD_DIM) for ref in refs], axis=0)

    def scores(h):
        q = q_ref[..., h * HEAD_DIM:(h + 1) * HEAD_DIM].reshape(tq, HEAD_DIM)
        return _dot_t(q, window(k_refs, h))

    s_next = scores(0)
    for h in range(DIL_HEADS_PER_GROUP):
        sl = slice(h * HEAD_DIM, (h + 1) * HEAD_DIM)
        s = s_next
        if h + 1 < DIL_HEADS_PER_GROUP:
            s_next = scores(h + 1)
        v = window(v_refs, h)
        s = s * scale - slope_ref[h] * dist
        s = jnp.where(valid, s, NEG_INF)
        m = jnp.max(s, axis=-1, keepdims=True)
        p = jnp.exp(s - m)
        den = jnp.sum(p, axis=-1, keepdims=True)
        o = _dot(p.astype(BF16), v) * (1.0 / den)
        o_ref[..., sl] = o.reshape(o_ref.shape[:-1] + (HEAD_DIM,))
        lse_ref[..., sl] = jnp.broadcast_to(m + jnp.log(den), (tq, HEAD_DIM)).reshape(o_ref.shape[:-1] + (HEAD_DIM,))


def _dil_group_attention(qkv, slopes, group, b, s):
    window, dil = DIL_GROUPS[group]
    radius = (window // 2) // dil
    n = s // dil
    tq = min(DIL_TQ, n)
    kb = tq // 2
    nblk = n // tq
    assert radius <= kb and s % IN_TM == 0 and n % tq == 0
    t = b * s
    rows = IN_TM // dil
    tiles = s // IN_TM
    q_unit = QKV_TILE0 + NA_TILES + 3 * group

    def spec(br, unit, pick):
        if rows >= br:
            sub = rows // br
            block = (None, None, br, DIL_OUT)
            index = lambda bi, r, u: (bi * tiles + u // sub, r, u % sub, 0)
        else:
            span = br // rows
            assert tiles % span == 0
            block = (span, None, rows, DIL_OUT)
            index = lambda bi, r, u: (bi * tiles // span + u, r, 0, 0)
        if unit is None:
            return pl.BlockSpec(block, lambda bi, r, i: index(bi, r, pick(i)))
        return pl.BlockSpec((None,) + block, lambda bi, r, i: (unit,) + index(bi, r, pick(i)))

    def kv_specs(unit):
        return [spec(kb, unit, lambda i, j=j: jnp.clip(2 * i - 1 + j, 0, n // kb - 1)) for j in range(DIL_KBLOCKS)]

    qkv_v = qkv.reshape(BF_TILES, t // IN_TM, dil, rows, IN_TN)
    kern = functools.partial(_dil_kernel, n=n, radius=radius, tq=tq)
    out_spec = spec(tq, None, lambda i: i)
    o, lse = pl.pallas_call(
        kern,
        grid=(b, dil, nblk),
        in_specs=[pl.BlockSpec(memory_space=pltpu.SMEM), spec(tq, q_unit, lambda i: i)]
        + kv_specs(q_unit + 1) + kv_specs(q_unit + 2),
        out_specs=[out_spec, out_spec],
        out_shape=[jax.ShapeDtypeStruct((t // IN_TM, dil, rows, DIL_OUT), F32)] * 2,
        compiler_params=_params("parallel", "parallel", "arbitrary"),
        name=f"dil_attn_g{group}",
    )(slopes, *([qkv_v] * (1 + 2 * DIL_KBLOCKS)))
    return o, lse


MERGE_TM = 512
NCH = DIL_OUT // LANES


def _merge_kernel(ona_ref, omla_ref, od0_ref, od1_ref, od2_ref, ls0_ref, ls1_ref, ls2_ref,
                  gna_ref, gmla_ref, gdil_ref, wna_ref, wmla_ref, wdil_ref, o_ref, od_ref, tok_ref):
    tm = MERGE_TM
    for a, (ref, dil) in enumerate(((od1_ref, 4), (ls1_ref, 4), (od2_ref, 16), (ls2_ref, 16))):
        for r in range(dil):
            for c in range(NCH):
                tok_ref[a, c, pl.ds(r, tm // dil, stride=dil), :] = ref[r, :, c * LANES:(c + 1) * LANES]
    for c in range(NCH):
        cs = slice(c * LANES, (c + 1) * LANES)
        od0, ls0 = od0_ref[0, :, cs], ls0_ref[0, :, cs]
        od1, ls1, od2, ls2 = tok_ref[0, c], tok_ref[1, c], tok_ref[2, c], tok_ref[3, c]
        top = jnp.maximum(jnp.maximum(ls0, ls1), ls2)
        w0, w1, w2 = jnp.exp(ls0 - top), jnp.exp(ls1 - top), jnp.exp(ls2 - top)
        inv = 1.0 / (w0 + w1 + w2)
        od = (w0 * inv) * od0 + (w1 * inv) * od1 + (w2 * inv) * od2
        od_ref[:, cs] = od.astype(BF16)

    o_na, o_mla, o_dil = ona_ref[...], omla_ref[...], od_ref[...]
    tn = wna_ref.shape[-1]
    for j in range(wna_ref.shape[0]):
        merged = gna_ref[j].astype(F32) * _dot(o_na, wna_ref[j])
        merged += gmla_ref[j].astype(F32) * _dot(o_mla, wmla_ref[j])
        merged += gdil_ref[j].astype(F32) * _dot(o_dil, wdil_ref[j])
        o_ref[:, j * tn:(j + 1) * tn] = merged.astype(o_ref.dtype)


def _col_tiles(w, tn):
    depth, k, n = w.shape
    return w.reshape(depth, k, n // tn, tn).transpose(0, 2, 1, 3)


def _merge(o_na, o_mla, o_dil, lse_dil, gates, w_na, w_mla, w_dil, layer):
    t = o_na.shape[0]
    tm = MERGE_TM
    tn = w_na.shape[-1]
    halves = IN_TM // tm
    gstep = D_MODEL // tn
    row = lambda width: pl.BlockSpec((tm, width), lambda i: (i, 0))
    gate = lambda g: pl.BlockSpec((gstep, tm, tn), lambda i: (g, i, 0))
    weight = lambda w: pl.BlockSpec((None,) + w.shape[1:], lambda i: (layer, 0, 0, 0), pipeline_mode=pl.Buffered(1))

    def dil_spec(g):
        dil = DIL_GROUPS[g][1]
        return pl.BlockSpec((None, dil, tm // dil, DIL_OUT), lambda i: (i // halves, 0, i % halves, 0))

    return pl.pallas_call(
        _merge_kernel,
        grid=(t // tm,),
        in_specs=[row(NA_WIDTH), row(MLA_OUT)] + [dil_spec(g) for g in range(3)] * 2
        + [gate(0), gate(1), gate(2)] + [weight(w) for w in (w_na, w_mla, w_dil)],
        out_specs=pl.BlockSpec((tm, D_MODEL), lambda i: (i, 0)),
        out_shape=jax.ShapeDtypeStruct((t, D_MODEL), BF16),
        scratch_shapes=[pltpu.VMEM((tm, DIL_OUT), BF16), pltpu.VMEM((4, NCH, tm, LANES), F32)],
        compiler_params=_params("parallel"),
        name="merge",
    )(o_na, o_mla, *o_dil, *lse_dil, gates, gates, gates, w_na, w_mla, w_dil)


def _resid_matmul_kernel(a_ref, w_ref, x_ref, o_ref):
    a = a_ref[...]
    tn = w_ref.shape[-1]
    for j in range(w_ref.shape[0]):
        cs = slice(j * tn, (j + 1) * tn)
        o_ref[:, cs] = x_ref[:, cs] + _dot(a, w_ref[j])


def _resid_matmul(a, w, x, layer, *, tm=1024):
    t, k = a.shape
    n = w.shape[1] * w.shape[-1]
    return pl.pallas_call(
        _resid_matmul_kernel,
        grid=(t // tm,),
        in_specs=[
            pl.BlockSpec((tm, k), lambda i: (i, 0)),
            pl.BlockSpec((None,) + w.shape[1:], lambda i: (layer, 0, 0, 0), pipeline_mode=pl.Buffered(1)),
            pl.BlockSpec((tm, n), lambda i: (i, 0)),
        ],
        out_specs=pl.BlockSpec((tm, n), lambda i: (i, 0)),
        out_shape=jax.ShapeDtypeStruct((t, n), F32),
        compiler_params=_params("parallel"),
        name="wo_resid",
    )(a, w, x)


def _prepare(p):
    depth = p["w_in"].shape[0]
    row = lambda g: g[:, None, :].astype(F32)
    w_in = p["w_in"]
    o = np.cumsum((0, NA_WIDTH, NA_WIDTH, NA_WIDTH, MLA_Q_RANK, MLA_KV_RANK, MLA_ROPE,
                   DIL_WIDTH, DIL_WIDTH, DIL_WIDTH, D_MODEL, D_MODEL, D_MODEL))
    seg = lambda a: w_in[:, :, o[a]:o[a + 1]]
    grp = lambda a, g: seg(a)[:, :, g * DIL_OUT:(g + 1) * DIL_OUT]
    rot = np.concatenate([np.arange(MLA_ROPE // 2, MLA_ROPE), np.arange(MLA_ROPE // 2)])
    zpad = jnp.zeros((depth, D_MODEL, LANES - MLA_ROPE), w_in.dtype)
    cols = [seg(9), seg(10), seg(11), seg(0), seg(1), seg(2)]
    for g in range(len(DIL_GROUPS)):
        cols += [grp(6, g), grp(7, g), grp(8, g)]
    cols += [seg(3), seg(4), seg(5), zpad, seg(5)[:, :, rot], zpad]
    w_all = jnp.concatenate(cols, axis=-1).astype(BF16)

    tile = lambda g, reps: jnp.tile(g, (1, reps))
    ones = lambda n: jnp.ones((depth, n), F32)
    dil_gain = jnp.concatenate([tile(p["dil_q_norm"], DIL_HEADS_PER_GROUP), tile(p["dil_k_norm"], DIL_HEADS_PER_GROUP),
                                ones(DIL_OUT)], axis=-1)
    head_gain = jnp.concatenate(
        [tile(p["na_q_norm"], NA_HEADS), tile(p["na_k_norm"], NA_HEADS), ones(NA_WIDTH)]
        + [dil_gain] * len(DIL_GROUPS), axis=-1)

    wuq = p["mla_w_uq"].reshape(depth, MLA_Q_RANK, MLA_HEADS, MLA_QK)
    zq = jnp.zeros((depth, MLA_Q_RANK, MLA_HEADS, LANES - MLA_ROPE), wuq.dtype)
    wuq_rope = wuq[..., MLA_NOPE:]
    wuq = jnp.concatenate([wuq[..., :MLA_NOPE], wuq_rope, zq, wuq_rope[..., rot], zq], axis=-1)
    pad_gain = lambda g: jnp.concatenate([g, jnp.zeros((depth, MLA_QPAD - MLA_QK), F32)], axis=-1)

    return dict(
        ffn1=(row(p["ffn1_norm"]), p["ffn1_w_gate"].astype(BF16), p["ffn1_w_up"].astype(BF16),
              p["ffn1_w_down"].astype(BF16)),
        ffn2=(row(p["ffn2_norm"]), p["ffn2_w_gate"].astype(BF16), p["ffn2_w_up"].astype(BF16),
              p["ffn2_w_down"].astype(BF16)),
        mix_norm=row(p["mix_norm"]),
        w_all=w_all, head_gain=row(head_gain),
        na_bias=[_na_bias_tables(p["na_rpb"][l]) for l in range(depth)],
        cq_gain=row(p["mla_cq_norm"]), ckv_gain=row(p["mla_ckv_norm"]),
        wuq=wuq.reshape(depth, MLA_Q_RANK, MLA_HEADS * MLA_QEXT).astype(BF16),
        wukv=p["mla_w_ukv"].astype(BF16),
        q_gain=row(pad_gain(p["mla_q_norm"]) * ((MLA_QK ** -0.5) * LOG2E)), k_gain=row(pad_gain(p["mla_k_norm"])),
        w_na=_col_tiles(p["w_na_out"].astype(BF16), IN_TN), w_mla=_col_tiles(p["w_mla_out"].astype(BF16), IN_TN),
        w_dil=_col_tiles(p["w_dil_out"].astype(BF16), IN_TN), w_o=_col_tiles(p["w_o"].astype(BF16), IN_TN),
    )


def _dil_slopes():
    slopes = _alibi_slopes(DIL_HEADS)
    out = []
    for g, (_, dil) in enumerate(DIL_GROUPS):
        lo, hi = g * DIL_HEADS_PER_GROUP, (g + 1) * DIL_HEADS_PER_GROUP
        out.append(slopes[lo:hi] * dil)
    return out


def _layer(x, w, layer, b, s, rope, slopes):
    x = _ffn(x, *w["ffn1"], layer)
    qkv, aux = _inproj(x, w["mix_norm"], w["w_all"], w["head_gain"], layer)
    o_na = _na_attention(qkv, w["na_bias"][layer], b, s)
    q, k, v = _mla_prep(aux, rope[0], rope[1], w["cq_gain"], w["ckv_gain"], w["wuq"], w["wukv"],
                        w["q_gain"], w["k_gain"], layer, s)
    o_mla = _mla_attention(q, k, v, b, s)
    dil = [_dil_group_attention(qkv, slopes[g], g, b, s) for g in range(len(DIL_GROUPS))]
    merged = _merge(o_na, o_mla, [d[0] for d in dil], [d[1] for d in dil], qkv,
                    w["w_na"], w["w_mla"], w["w_dil"], layer)
    x = _resid_matmul(merged, w["w_o"], x, layer)
    return _ffn(x, *w["ffn2"], layer)


def _trunk(x, w, slopes):
    b, s, d = x.shape
    rope = _rope_tables(s)
    y = x.reshape(b * s, d)
    for layer in range(w["w_all"].shape[0]):
        y = _layer(y, w, layer, b, s, rope, slopes)
    return y.reshape(b, s, d)


def kernel(x_prompt, x_sample, ffn1_norm, ffn1_w_gate, ffn1_w_up, ffn1_w_down, mix_norm, w_in, na_q_norm, na_k_norm, na_rpb, mla_cq_norm, mla_ckv_norm, mla_w_uq, mla_w_ukv, mla_q_norm, mla_k_norm, dil_q_norm, dil_k_norm, w_na_out, w_mla_out, w_dil_out, w_o, ffn2_norm, ffn2_w_gate, ffn2_w_up, ffn2_w_down):
    w = _prepare(dict(
        ffn1_norm=ffn1_norm, ffn1_w_gate=ffn1_w_gate, ffn1_w_up=ffn1_w_up, ffn1_w_down=ffn1_w_down,
        mix_norm=mix_norm, w_in=w_in, na_q_norm=na_q_norm, na_k_norm=na_k_norm, na_rpb=na_rpb,
        mla_cq_norm=mla_cq_norm, mla_ckv_norm=mla_ckv_norm, mla_w_uq=mla_w_uq, mla_w_ukv=mla_w_ukv,
        mla_q_norm=mla_q_norm, mla_k_norm=mla_k_norm, dil_q_norm=dil_q_norm, dil_k_norm=dil_k_norm,
        w_na_out=w_na_out, w_mla_out=w_mla_out, w_dil_out=w_dil_out, w_o=w_o,
        ffn2_norm=ffn2_norm, ffn2_w_gate=ffn2_w_gate, ffn2_w_up=ffn2_w_up, ffn2_w_down=ffn2_w_down))
    slopes = _dil_slopes()
    return (_trunk(x_prompt, w, slopes), _trunk(x_sample, w, slopes))
```
